```python
import math
import jax
import jax.numpy as jnp
from jax import lax
import numpy as np

D_MODEL = 1024
BATCH = 8
SEQ = 8192
DEPTH = 2

GRID_W = 64
CTX_LEN = 256
N_MIXERS = 2
EPS = 1e-6
MOD_CHUNKS = 6

HY_SHORT = 3
HY_BANDS = 16
HY_EMB_DIM = 1 + 2 * HY_BANDS
HY_FILTER_HIDDEN = 64
HY_DECAY_TARGET = 1e-2
HY_FAST_DECAY_PCT = 0.3
HY_SLOW_DECAY_PCT = 1.5
HY_MAX_DECAY = math.log(HY_DECAY_TARGET) / HY_FAST_DECAY_PCT
HY_MIN_DECAY = math.log(HY_DECAY_TARGET) / HY_SLOW_DECAY_PCT

N_HEADS = 16
N_KV_HEADS = 4
GROUP = N_HEADS // N_KV_HEADS
HEAD_DIM = D_MODEL // N_HEADS
Q_WIDTH = N_HEADS * HEAD_DIM
KV_WIDTH = N_KV_HEADS * HEAD_DIM
QKV_WIDTH = Q_WIDTH + 2 * KV_WIDTH
WINDOW = 128
BLOCK = 128
ROPE_BASE = 10000.0
ROPE_AXIS_DIM = HEAD_DIM // 2
ROPE_NFREQ = ROPE_AXIS_DIM // 2
ATTN_SCALE = HEAD_DIM ** -0.5

D_FF = 4 * D_MODEL

kernel_name = 'hybrid_hyena_swa_sink_dit'


def _rmsnorm(x, g):
    xf = x.astype(jnp.float32)
    y = xf * lax.rsqrt(jnp.mean(xf * xf, axis=-1, keepdims=True) + EPS)
    return (y * g.astype(jnp.float32)).astype(x.dtype)


def _modulate(h, shift, scale):
    return h * (1 + scale) + shift


def _mlp(h, w1, w2):
    return jnp.square(jax.nn.relu(h @ w1)) @ w2


def _short_conv(x, w, b):
    L = x.shape[1]
    pad = HY_SHORT // 2
    xp = jnp.pad(x, ((0, 0), (pad, HY_SHORT - 1 - pad), (0, 0)))
    out = b
    for k in range(HY_SHORT):
        out = out + xp[:, k:k + L] * w[k]
    return out


def _hyena_kernel(L, w1, b1, fr1, w2, b2, fr2, w3):
    f32 = jnp.float32
    pos = jnp.arange(L, dtype=f32)
    t = pos / L
    bands = jnp.linspace(1e-4, HY_BANDS - 1, HY_BANDS, dtype=f32)
    ang = (2.0 * math.pi / L) * pos[:, None] * bands[None, :]
    z = jnp.concatenate([t[:, None], jnp.cos(ang), -jnp.sin(ang)], axis=-1)
    h = jnp.sin(fr1.astype(f32) * (z @ w1.astype(f32) + b1.astype(f32)))
    h = jnp.sin(fr2.astype(f32) * (h @ w2.astype(f32) + b2.astype(f32)))
    h = h @ w3.astype(f32)
    deltas = jnp.abs(jnp.linspace(HY_MIN_DECAY, HY_MAX_DECAY, D_MODEL, dtype=f32))
    decay = jnp.exp(-t[:, None] * deltas[None, :])
    h_fwd = h[:, :D_MODEL] * decay
    h_bwd = h[:, D_MODEL:] * decay
    kc = jnp.concatenate([h_fwd, jnp.zeros((1, D_MODEL), f32), h_bwd[:0:-1]], axis=0)
    return kc / jnp.sum(jnp.abs(kc), axis=0, keepdims=True)


def _hyena_mixer(h, w_in, b_in, conv_w, conv_b, f_w1, f_b1, f_fr1, f_w2, f_b2, f_fr2, f_w3,
                 skip, w_out, b_out):
    L = h.shape[1]
    proj = _short_conv(h @ w_in + b_in, conv_w, conv_b)
    x0, x1, v = jnp.split(proj, 3, axis=-1)
    u = (x1 * v).astype(jnp.float32)
    kc = _hyena_kernel(L, f_w1, f_b1, f_fr1, f_w2, f_b2, f_fr2, f_w3)
    U = jnp.fft.rfft(u, n=2 * L, axis=1)
    K = jnp.fft.rfft(kc, axis=0)
    y = jnp.fft.irfft(U * K[None], n=2 * L, axis=1)[:, :L] + u * skip.astype(jnp.float32)
    y = x0 * y.astype(h.dtype)
    return y @ w_out + b_out


def _axial_rope_tables(L):
    rows = L // GRID_W
    row = jnp.repeat(jnp.arange(rows, dtype=jnp.float32), GRID_W)
    col = jnp.tile(jnp.arange(GRID_W, dtype=jnp.float32), rows)
    inv = ROPE_BASE ** (-jnp.arange(ROPE_NFREQ, dtype=jnp.float32) / ROPE_NFREQ)
    ang_r = row[:, None] * inv[None, :]
    ang_c = col[:, None] * inv[None, :]
    return jnp.cos(ang_r), jnp.sin(ang_r), jnp.cos(ang_c), jnp.sin(ang_c)


def _rotate(x, cos, sin):
    x1 = x[..., :ROPE_NFREQ]
    x2 = x[..., ROPE_NFREQ:]
    cs = cos[None, :, None, :]
    sn = sin[None, :, None, :]
    return jnp.concatenate([x1 * cs - x2 * sn, x2 * cs + x1 * sn], axis=-1)


def _apply_axial_rope(x, tables):
    cr, sr, cc, sc = tables
    xf = x.astype(jnp.float32)
    out = jnp.concatenate([_rotate(xf[..., :ROPE_AXIS_DIM], cr, sr),
                           _rotate(xf[..., ROPE_AXIS_DIM:], cc, sc)], axis=-1)
    return out.astype(x.dtype)


def _qkv(h, w, b, qn, kn, with_q):
    B, L, _ = h.shape
    if with_q:
        qkv = h @ w + b
        q, k, v = jnp.split(qkv, [Q_WIDTH, Q_WIDTH + KV_WIDTH], axis=-1)
        q = _rmsnorm(q.reshape(B, L, N_HEADS, HEAD_DIM), qn)
    else:
        kv = h @ w[:, Q_WIDTH:] + b[Q_WIDTH:]
        k, v = jnp.split(kv, 2, axis=-1)
        q = None
    k = _rmsnorm(k.reshape(B, L, N_KV_HEADS, HEAD_DIM), kn)
    v = v.reshape(B, L, N_KV_HEADS, HEAD_DIM)
    return q, k, v


def _sink_softmax(scores, sink):
    s = jnp.broadcast_to(sink[None, :, :, None, None], scores.shape[:-1] + (1,))
    p = jax.nn.softmax(jnp.concatenate([scores, s], axis=-1), axis=-1)
    return p[..., :-1]


def _context_attention(qc, kc, vc, sink):
    B, Lc = qc.shape[:2]
    qg = qc.reshape(B, Lc, N_KV_HEADS, GROUP, HEAD_DIM)
    s = jnp.einsum('bqhgd,bkhd->bhgqk', qg, kc, preferred_element_type=jnp.float32) * ATTN_SCALE
    p = _sink_softmax(s, sink)
    o = jnp.einsum('bhgqk,bkhd->bqhgd', p.astype(vc.dtype), vc)
    return o.reshape(B, Lc, Q_WIDTH)


def _latent_window_attention(q, k, v, kc, vc, sink):
    B, L = q.shape[:2]
    Lc = kc.shape[1]
    nb = L // BLOCK
    span = BLOCK + 2 * WINDOW
    qg = q.reshape(B, L, N_KV_HEADS, GROUP, HEAD_DIM)
    kp = jnp.pad(k, ((0, 0), (WINDOW, WINDOW), (0, 0), (0, 0)))
    vp = jnp.pad(v, ((0, 0), (WINDOW, WINDOW), (0, 0), (0, 0)))
    rel = jnp.arange(span)[None, :] - WINDOW - jnp.arange(BLOCK)[:, None]
    band = jnp.abs(rel) <= WINDOW

    def one_block(b):
        start = b * BLOCK
        qb = lax.dynamic_slice_in_dim(qg, start, BLOCK, axis=1)
        kb = lax.dynamic_slice_in_dim(kp, start, span, axis=1)
        vb = lax.dynamic_slice_in_dim(vp, start, span, axis=1)
        kpos = start - WINDOW + jnp.arange(span)
        valid = band & ((kpos >= 0) & (kpos < L))[None, :]
        s_w = jnp.einsum('bqhgd,bkhd->bhgqk', qb, kb, preferred_element_type=jnp.float32) * ATTN_SCALE
        s_w = jnp.where(valid, s_w, -jnp.inf)
        s_c = jnp.einsum('bqhgd,bkhd->bhgqk', qb, kc, preferred_element_type=jnp.float32) * ATTN_SCALE
        p = _sink_softmax(jnp.concatenate([s_w, s_c], axis=-1), sink)
        p_w = p[..., :span].astype(vb.dtype)
        p_c = p[..., span:span + Lc].astype(vc.dtype)
        o = (jnp.einsum('bhgqk,bkhd->bqhgd', p_w, vb)
             + jnp.einsum('bhgqk,bkhd->bqhgd', p_c, vc))
        return o.reshape(B, BLOCK, Q_WIDTH)

    out = lax.map(one_block, jnp.arange(nb))
    return jnp.transpose(out, (1, 0, 2, 3)).reshape(B, L, Q_WIDTH)


def setup_inputs(seed: int = 0) -> dict:
    key = jax.random.key(seed)
    ks = jax.random.split(key, 40)
    n_hy = (DEPTH + N_MIXERS - 1) // N_MIXERS
    n_at = DEPTH // N_MIXERS
    f32 = jnp.float32

    def nrm(i, shape, scale):
        return jax.random.normal(ks[i], shape, f32) * scale

    return {
        'x': nrm(0, (BATCH, SEQ, D_MODEL), 1.0),
        'c': nrm(1, (BATCH, D_MODEL), 1.0),
        'ctx': nrm(2, (BATCH, CTX_LEN, D_MODEL), 1.0),
        'c_ctx': nrm(3, (D_MODEL,), 1.0),
        'mod_w': nrm(4, (DEPTH, D_MODEL, MOD_CHUNKS * D_MODEL), 0.5 * D_MODEL ** -0.5),
        'mod_b': nrm(5, (DEPTH, MOD_CHUNKS * D_MODEL), 0.02),
        'norm1_w': 1.0 + nrm(6, (DEPTH, D_MODEL), 0.1),
        'norm2_w': 1.0 + nrm(7, (DEPTH, D_MODEL), 0.1),
        'mlp_w1': nrm(8, (DEPTH, D_MODEL, D_FF), D_MODEL ** -0.5),
        'mlp_w2': nrm(9, (DEPTH, D_FF, D_MODEL), D_FF ** -0.5),
        'hy_w_in': nrm(10, (n_hy, D_MODEL, 3 * D_MODEL), D_MODEL ** -0.5),
        'hy_b_in': nrm(11, (n_hy, 3 * D_MODEL), 0.02),
        'hy_conv_w': nrm(12, (n_hy, HY_SHORT, 3 * D_MODEL), HY_SHORT ** -0.5),
        'hy_conv_b': nrm(13, (n_hy, 3 * D_MODEL), 0.02),
        'hy_f_w1': nrm(14, (n_hy, HY_EMB_DIM, HY_FILTER_HIDDEN), HY_EMB_DIM ** -0.5),
        'hy_f_b1': nrm(15, (n_hy, HY_FILTER_HIDDEN), 0.2),
        'hy_f_freq1': 1.0 + nrm(16, (n_hy, HY_FILTER_HIDDEN), 0.01),
        'hy_f_w2': nrm(17, (n_hy, HY_FILTER_HIDDEN, HY_FILTER_HIDDEN), HY_FILTER_HIDDEN ** -0.5),
        'hy_f_b2': nrm(18, (n_hy, HY_FILTER_HIDDEN), 0.2),
        'hy_f_freq2': 1.0 + nrm(19, (n_hy, HY_FILTER_HIDDEN), 0.01),
        'hy_f_w3': nrm(20, (n_hy, HY_FILTER_HIDDEN, 2 * D_MODEL), HY_FILTER_HIDDEN ** -0.5),
        'hy_skip': nrm(21, (n_hy, D_MODEL), 1.0),
        'hy_w_out': nrm(22, (n_hy, D_MODEL, D_MODEL), D_MODEL ** -0.5),
        'hy_b_out': nrm(23, (n_hy, D_MODEL), 0.02),
        'at_w_qkv': nrm(24, (n_at, D_MODEL, QKV_WIDTH), D_MODEL ** -0.5),
        'at_b_qkv': nrm(25, (n_at, QKV_WIDTH), 0.02),
        'at_q_norm': 1.0 + nrm(26, (n_at, HEAD_DIM), 0.1),
        'at_k_norm': 1.0 + nrm(27, (n_at, HEAD_DIM), 0.1),
        'at_sink': nrm(28, (n_at, N_HEADS), 0.5),
        'at_w_out': nrm(29, (n_at, Q_WIDTH, D_MODEL), Q_WIDTH ** -0.5),
        'at_b_out': nrm(30, (n_at, D_MODEL), 0.02),
    }


def reference(x, c, ctx, c_ctx, mod_w, mod_b, norm1_w, norm2_w, mlp_w1, mlp_w2,
              hy_w_in, hy_b_in, hy_conv_w, hy_conv_b, hy_f_w1, hy_f_b1, hy_f_freq1,
              hy_f_w2, hy_f_b2, hy_f_freq2, hy_f_w3, hy_skip, hy_w_out, hy_b_out,
              at_w_qkv, at_b_qkv, at_q_norm, at_k_norm, at_sink, at_w_out, at_b_out):
    L = x.shape[1]
    rope_tables = _axial_rope_tables(L)
    for i in range(DEPTH):
        last = i == DEPTH - 1
        kind = i % N_MIXERS
        j = i // N_MIXERS
        need_ctx = (not last) or kind == 1
        mod = jax.nn.silu(c) @ mod_w[i] + mod_b[i]
        m = jnp.split(mod[:, None, :], MOD_CHUNKS, axis=-1)
        hx = _modulate(_rmsnorm(x, norm1_w[i]), m[0], m[1])
        if need_ctx:
            mod_c = jax.nn.silu(c_ctx) @ mod_w[i] + mod_b[i]
            mc = jnp.split(mod_c[None, None, :], MOD_CHUNKS, axis=-1)
            hc = _modulate(_rmsnorm(ctx, norm1_w[i]), mc[0], mc[1])
        if kind == 0:
            hp = (hy_w_in[j], hy_b_in[j], hy_conv_w[j], hy_conv_b[j], hy_f_w1[j], hy_f_b1[j],
                  hy_f_freq1[j], hy_f_w2[j], hy_f_b2[j], hy_f_freq2[j], hy_f_w3[j], hy_skip[j],
                  hy_w_out[j], hy_b_out[j])
            dx = _hyena_mixer(hx, *hp)
            if not last:
                dc = _hyena_mixer(hc, *hp)
        else:
            sink = at_sink[j].astype(jnp.float32).reshape(N_KV_HEADS, GROUP)
            qx, kx, vx = _qkv(hx, at_w_qkv[j], at_b_qkv[j], at_q_norm[j], at_k_norm[j], True)
            qx = _apply_axial_rope(qx, rope_tables)
            kx = _apply_axial_rope(kx, rope_tables)
            qc, kc, vc = _qkv(hc, at_w_qkv[j], at_b_qkv[j], at_q_norm[j], at_k_norm[j], not last)
            dx = _latent_window_attention(qx, kx, vx, kc, vc, sink) @ at_w_out[j] + at_b_out[j]
            if not last:
                dc = _context_attention(qc, kc, vc, sink) @ at_w_out[j] + at_b_out[j]
        x = x + m[2] * dx
        x = x + m[5] * _mlp(_modulate(_rmsnorm(x, norm2_w[i]), m[3], m[4]), mlp_w1[i], mlp_w2[i])
        if not last:
            ctx = ctx + mc[2] * dc
            ctx = ctx + mc[5] * _mlp(_modulate(_rmsnorm(ctx, norm2_w[i]), mc[3], mc[4]),
                                     mlp_w1[i], mlp_w2[i])
    return x
```

```python
import functools
import math

import numpy as np
import jax
import jax.numpy as jnp
from jax import lax
from jax.experimental import pallas as pl
from jax.experimental.pallas import tpu as pltpu

F32 = jnp.float32
BF16 = jnp.bfloat16

EPS = 1e-6
MOD_CHUNKS = 6
GRID_W = 64
HY_SHORT = 3
HY_BANDS = 16
HY_DECAY_TARGET = 1e-2
HY_MAX_DECAY = math.log(HY_DECAY_TARGET) / 0.3
HY_MIN_DECAY = math.log(HY_DECAY_TARGET) / 1.5
N_HEADS = 16
N_KV_HEADS = 4
GROUP = N_HEADS // N_KV_HEADS
HEAD_DIM = 64
WINDOW = 128
ROPE_BASE = 10000.0
ROPE_NFREQ = HEAD_DIM // 4

V7X_VMEM_BYTES = 64 * 1024 * 1024
VMEM_LIMIT = V7X_VMEM_BYTES - 8 * 1024 * 1024
LANES = 128
FFT_N2 = 128
NEG_BIG = -1e30


def _params(n_axes):
    return pltpu.CompilerParams(dimension_semantics=("arbitrary",) * n_axes,
                                vmem_limit_bytes=VMEM_LIMIT)


def _split_bf16(a):
    hi = a.astype(BF16)
    lo = (a - hi.astype(F32)).astype(BF16)
    return hi, lo


def _dot(a, b):
    return jnp.dot(a, b, preferred_element_type=F32)


def _dot3(a, b):
    ah, al = _split_bf16(a)
    bh, bl = _split_bf16(b)
    return _dot(ah, bh) + (_dot(ah, bl) + _dot(al, bh))


def _norm_mod(x, g, shift, scale):
    ms = jnp.mean(x * x, axis=-1, keepdims=True)
    y = x * lax.rsqrt(ms + EPS) * g
    return y * (1.0 + scale) + shift


def _full(shape):
    nd = len(shape)
    return pl.BlockSpec(shape, lambda *_: (0,) * nd)


def _mod_kernel(c_ref, w_ref, b_ref, o_ref):
    c = c_ref[...]
    s = c * jax.nn.sigmoid(c)
    o_ref[...] = _dot3(s, w_ref[...]) + b_ref[...]


def _modulation(cc, mod_w, mod_b):
    depth, d, n = mod_w.shape
    r = cc.shape[0]
    tn = 1536
    out = pl.pallas_call(
        _mod_kernel,
        grid=(depth, n // tn),
        in_specs=[pl.BlockSpec((r, d), lambda i, j: (0, 0)),
                  pl.BlockSpec((None, d, tn), lambda i, j: (i, 0, j)),
                  pl.BlockSpec((None, 1, tn), lambda i, j: (i, 0, j))],
        out_specs=pl.BlockSpec((None, r, tn), lambda i, j: (i, 0, j)),
        out_shape=jax.ShapeDtypeStruct((depth, r, n), F32),
        compiler_params=_params(2),
        name="modulation",
    )(cc, mod_w, mod_b.reshape(depth, 1, n))
    return out.reshape(depth, r, MOD_CHUNKS, d)


def _inproj_kernel(x_ref, xp_ref, xn_ref, mod_ref, g_ref, w_ref, b_ref, cw_ref, cb_ref,
                   x0_ref, u_ref, *, tm, tiles_per_seq, halo, cw_cols):
    t = pl.program_id(0)
    first = (t % tiles_per_seq) == 0
    last = (t % tiles_per_seq) == tiles_per_seq - 1
    g = g_ref[...]
    shift = mod_ref[0:1, :]
    scale = mod_ref[1:2, :]

    def nm(v):
        return _norm_mod(v, g, shift, scale).astype(BF16)

    h = jnp.concatenate([nm(xp_ref[...]), nm(x_ref[...]), nm(xn_ref[...])], axis=0)
    d = x_ref.shape[1]
    row = lax.broadcasted_iota(jnp.int32, (halo, cw_cols), 0)
    kill_top = jnp.logical_and(first, row == halo - 1)
    kill_bot = jnp.logical_and(last, row == 0)

    def conv(col0):
        p = _dot(h, w_ref[:, col0:col0 + cw_cols]) + b_ref[:, col0:col0 + cw_cols]
        top = jnp.where(kill_top, 0.0, p[0:halo])
        bot = jnp.where(kill_bot, 0.0, p[tm + halo:tm + 2 * halo])
        pm = jnp.concatenate([top, p[halo:tm + halo], bot], axis=0)
        out = cb_ref[:, col0:col0 + cw_cols]
        for k in range(HY_SHORT):
            out = out + pm[halo - 1 + k:halo - 1 + k + tm] * cw_ref[k:k + 1, col0:col0 + cw_cols]
        return out

    for c0 in range(0, d, cw_cols):
        x0_ref[:, c0:c0 + cw_cols] = conv(c0).astype(BF16)
        u_ref[:, c0:c0 + cw_cols] = (conv(d + c0) * conv(2 * d + c0)).astype(BF16)


def _hyena_inproj(x2, mod_l, mod_index, seq_len, g, w_bf, b, conv_w, conv_b):
    t_rows, d = x2.shape
    tm = min(512, seq_len)
    halo = 16
    tps = seq_len // tm
    nblk = t_rows // halo
    kern = functools.partial(_inproj_kernel, tm=tm, tiles_per_seq=tps, halo=halo, cw_cols=256)
    return pl.pallas_call(
        kern,
        grid=(t_rows // tm,),
        in_specs=[pl.BlockSpec((tm, d), lambda t: (t, 0)),
                  pl.BlockSpec((halo, d), lambda t: (jnp.maximum(t * (tm // halo) - 1, 0), 0)),
                  pl.BlockSpec((halo, d), lambda t: (jnp.minimum((t + 1) * (tm // halo), nblk - 1), 0)),
                  pl.BlockSpec((None, MOD_CHUNKS, d), lambda t: (mod_index(t, tps), 0, 0)),
                  _full((1, d)), _full((d, 3 * d)), _full((1, 3 * d)),
                  _full((HY_SHORT, 3 * d)), _full((1, 3 * d))],
        out_specs=[pl.BlockSpec((tm, d), lambda t: (t, 0)),
                   pl.BlockSpec((tm, d), lambda t: (t, 0))],
        out_shape=[jax.ShapeDtypeStruct((t_rows, d), BF16),
                   jax.ShapeDtypeStruct((t_rows, d), BF16)],
        compiler_params=_params(1),
        name="hyena_inproj",
    )(x2, x2, x2, mod_l, g.reshape(1, d), w_bf, b.reshape(1, 3 * d), conv_w, conv_b.reshape(1, 3 * d))


def _filter_hidden_kernel(z_ref, w1_ref, b1_ref, f1_ref, w2_ref, b2_ref, f2_ref, o_ref):
    h = jnp.sin(f1_ref[...] * (_dot3(z_ref[...], w1_ref[...]) + b1_ref[...]))
    o_ref[...] = jnp.sin(f2_ref[...] * (_dot3(h, w2_ref[...]) + b2_ref[...]))


def _filter_out_kernel(h_ref, w3_ref, dl_ref, o_ref, sum_ref, *, seq_len):
    phase = pl.program_id(1)
    i = pl.program_id(2)
    tr, td = o_ref.shape
    r = lax.broadcasted_iota(jnp.int32, (tr, td), 0) + i * tr
    pos = jnp.where(r < seq_len, r, 2 * seq_len - r)
    t = pos.astype(F32) / seq_len
    kc = _dot3(h_ref[...], w3_ref[...]) * jnp.exp(-t * dl_ref[...])
    kc = jnp.where(r == seq_len, 0.0, kc)

    @pl.when(jnp.logical_and(phase == 0, i == 0))
    def _():
        sum_ref[...] = jnp.zeros_like(sum_ref)

    @pl.when(phase == 0)
    def _():
        sum_ref[...] += jnp.sum(jnp.abs(kc), axis=0, keepdims=True)

    @pl.when(phase == 1)
    def _():
        o_ref[...] = kc / sum_ref[...]


def _filter_features(seq_len):
    f32 = jnp.float32
    pos_f = jnp.arange(seq_len, dtype=f32)
    pos_b = ((seq_len - jnp.arange(seq_len)) % seq_len).astype(f32)
    pos = jnp.concatenate([pos_f, pos_b])
    t = pos / seq_len
    bands = jnp.linspace(1e-4, HY_BANDS - 1, HY_BANDS, dtype=f32)
    ang = (2.0 * math.pi / seq_len) * pos[:, None] * bands[None, :]
    return jnp.concatenate([t[:, None], jnp.cos(ang), -jnp.sin(ang)], axis=-1)


def _hyena_filter(seq_len, d, f_w1, f_b1, f_fr1, f_w2, f_b2, f_fr2, f_w3):
    z = _filter_features(seq_len)
    emb = z.shape[1]
    hid = f_w1.shape[1]
    rows = 2 * seq_len
    tr = min(2048, seq_len)
    h2 = pl.pallas_call(
        _filter_hidden_kernel,
        grid=(rows // tr,),
        in_specs=[pl.BlockSpec((tr, emb), lambda i: (i, 0)),
                  _full((emb, hid)), _full((1, hid)), _full((1, hid)),
                  _full((hid, hid)), _full((1, hid)), _full((1, hid))],
        out_specs=pl.BlockSpec((tr, hid), lambda i: (i, 0)),
        out_shape=jax.ShapeDtypeStruct((rows, hid), F32),
        compiler_params=_params(1),
        name="hyena_filter_hidden",
    )(z, f_w1, f_b1.reshape(1, hid), f_fr1.reshape(1, hid), f_w2, f_b2.reshape(1, hid),
      f_fr2.reshape(1, hid))
    deltas = jnp.abs(jnp.linspace(HY_MIN_DECAY, HY_MAX_DECAY, d, dtype=F32)).reshape(1, d)
    td = 256
    nd = d // td
    tiles_per_half = seq_len // tr
    return pl.pallas_call(
        functools.partial(_filter_out_kernel, seq_len=seq_len),
        grid=(nd, 2, rows // tr),
        in_specs=[pl.BlockSpec((tr, hid), lambda j, p, i: (i, 0)),
                  pl.BlockSpec((hid, td), lambda j, p, i: (0, (i // tiles_per_half) * nd + j)),
                  pl.BlockSpec((1, td), lambda j, p, i: (0, j))],
        out_specs=pl.BlockSpec((tr, td), lambda j, p, i: (p * i, j)),
        out_shape=jax.ShapeDtypeStruct((rows, d), F32),
        scratch_shapes=[pltpu.VMEM((1, td), F32)],
        compiler_params=_params(3),
        name="hyena_filter_out",
    )(h2, f_w3, deltas)


def _dft_tables(n1):
    n2 = FFT_N2
    n = n1 * n2
    h1 = n1 // 2
    k1 = np.concatenate([np.arange(0, h1 + 1), np.arange(1, h1)])
    is_im = np.concatenate([np.zeros(h1 + 1, bool), np.ones(h1 - 1, bool)])
    th = 2.0 * np.pi * np.outer(k1, np.arange(n1)) / n1
    f1 = np.where(is_im[:, None], -np.sin(th), np.cos(th))
    wgt = np.where((k1 == 0) | (k1 == h1), 1.0, 2.0)[:, None] / n
    f1inv = (np.where(is_im[:, None], -np.sin(th), np.cos(th)) * wgt).T
    m = np.arange(n2)
    k2 = np.arange(n2)

    def g(k1v):
        ph = 2.0 * np.pi * np.outer(k2 * n1 + k1v, m) / n
        return np.cos(ph), -np.sin(ph)

    mf = np.zeros((h1, 2 * n2, 2 * n2))
    for j in range(1, h1):
        gr, gi = g(j)
        mf[j] = np.block([[gr, -gi], [gi, gr]])
    g0r, g0i = g(0)
    ghr, ghi = g(h1)
    zz = np.zeros((n2, n2))
    m0 = np.block([[g0r, zz], [g0i, zz], [zz, ghr], [zz, ghi]])
    mfi = np.transpose(mf, (0, 2, 1))
    m0i = m0.T
    as32 = lambda a: jnp.asarray(a.astype(np.float32))
    return dict(f1=as32(f1), f1inv=as32(f1inv), mf=as32(mf), mfi=as32(mfi), m0=as32(m0), m0i=as32(m0i))


def _stage1_kernel(f_ref, x_ref, o_ref):
    o_ref[...] = _dot(f_ref[...], x_ref[...].astype(BF16)).astype(o_ref.dtype)


def _dft_stage1(f_bf, x3, out_dtype):
    bx, r, c = x3.shape
    m = f_bf.shape[0]
    tc = min(8192, c)
    return pl.pallas_call(
        _stage1_kernel,
        grid=(bx, c // tc),
        in_specs=[_full((m, r)), pl.BlockSpec((None, r, tc), lambda b, j: (b, 0, j))],
        out_specs=pl.BlockSpec((None, m, tc), lambda b, j: (b, 0, j)),
        out_shape=jax.ShapeDtypeStruct((bx, m, c), out_dtype),
        compiler_params=_params(2),
        name="dft_stage1",
    )(f_bf, x3)


def _filter_spec_kernel(a_ref, mf_ref, m0_ref, ks_ref, kh_ref):
    j = pl.program_id(0)
    n2 = FFT_N2
    a = a_ref[...].reshape(2 * n2, a_ref.shape[-1])

    @pl.when(j == 0)
    def _():
        x = _dot(m0_ref[...], a)
        ks_ref[...] = x[0:2 * n2].reshape(ks_ref.shape)
        kh_ref[...] = x[2 * n2:4 * n2].reshape(kh_ref.shape)

    @pl.when(j > 0)
    def _():
        ks_ref[...] = _dot(mf_ref[...], a).reshape(ks_ref.shape)


def _filter_spectrum(ak, tabs):
    _, h1, n2, d = ak.shape
    return pl.pallas_call(
        _filter_spec_kernel,
        grid=(h1,),
        in_specs=[pl.BlockSpec((2, None, n2, d), lambda j: (0, j, 0, 0)),
                  pl.BlockSpec((None, 2 * n2, 2 * n2), lambda j: (j, 0, 0)),
                  _full((4 * n2, 2 * n2))],
        out_specs=[pl.BlockSpec((None, 2, n2, d), lambda j: (j, 0, 0, 0)),
                   _full((2, n2, d))],
        out_shape=[jax.ShapeDtypeStruct((h1, 2, n2, d), F32),
                   jax.ShapeDtypeStruct((2, n2, d), F32)],
        compiler_params=_params(1),
        name="hyena_filter_spectrum",
    )(ak, tabs["mf"].astype(BF16), tabs["m0"].astype(BF16))


def _cmul(xr, xi, kr, ki):
    return xr * kr - xi * ki, xr * ki + xi * kr


def _stage2_kernel(a_ref, mf_ref, mfi_ref, m0_ref, m0i_ref, ks_ref, kh_ref, o_ref):
    j = pl.program_id(0)
    n2 = FFT_N2
    nb = a_ref.shape[0]
    d = a_ref.shape[-1]

    @pl.when(j == 0)
    def _():
        for b in range(nb):
            a = a_ref[b].reshape(2 * n2, d)
            x = _dot(m0_ref[...], a)
            y0r, y0i = _cmul(x[0:n2], x[n2:2 * n2], ks_ref[0], ks_ref[1])
            y1r, y1i = _cmul(x[2 * n2:3 * n2], x[3 * n2:4 * n2], kh_ref[0], kh_ref[1])
            y = jnp.concatenate([y0r, y0i, y1r, y1i], axis=0).astype(BF16)
            o_ref[b] = _dot(m0i_ref[...], y).reshape(2, n2, d).astype(o_ref.dtype)

    @pl.when(j > 0)
    def _():
        for b in range(nb):
            a = a_ref[b].reshape(2 * n2, d)
            x = _dot(mf_ref[...], a)
            yr, yi = _cmul(x[0:n2], x[n2:2 * n2], ks_ref[0], ks_ref[1])
            y = jnp.concatenate([yr, yi], axis=0).astype(BF16)
            o_ref[b] = _dot(mfi_ref[...], y).reshape(2, n2, d).astype(o_ref.dtype)


def _dft_stage2(a5, tabs, ks, kh):
    nb, _, h1, n2, d = a5.shape
    blk = pl.BlockSpec((nb, 2, None, n2, d), lambda j: (0, 0, j, 0, 0))
    return pl.pallas_call(
        _stage2_kernel,
        grid=(h1,),
        in_specs=[blk,
                  pl.BlockSpec((None, 2 * n2, 2 * n2), lambda j: (j, 0, 0)),
                  pl.BlockSpec((None, 2 * n2, 2 * n2), lambda j: (j, 0, 0)),
                  _full((4 * n2, 2 * n2)), _full((2 * n2, 4 * n2)),
                  pl.BlockSpec((None, 2, n2, d), lambda j: (j, 0, 0, 0)),
                  _full((2, n2, d))],
        out_specs=blk,
        out_shape=jax.ShapeDtypeStruct(a5.shape, BF16),
        compiler_params=_params(1),
        name="dft_stage2",
    )(a5, tabs["mf"].astype(BF16), tabs["mfi"].astype(BF16), tabs["m0"].astype(BF16),
      tabs["m0i"].astype(BF16), ks, kh)


def _long_conv(u, kc):
    b, seq_len, d = u.shape
    n2 = FFT_N2
    n1 = 2 * seq_len // n2
    h1 = n1 // 2
    tabs = _dft_tables(n1)
    f1 = tabs["f1"].astype(BF16)
    ak = _dft_stage1(f1, kc.reshape(1, n1, n2 * d), BF16)
    ks, kh = _filter_spectrum(ak.reshape(2, h1, n2, d), tabs)
    a = _dft_stage1(f1[:, :h1], u.reshape(b, h1, n2 * d), BF16)
    bv = _dft_stage2(a.reshape(b, 2, h1, n2, d), tabs, ks, kh)
    y = _dft_stage1(tabs["f1inv"][:h1].astype(BF16), bv.reshape(b, n1, n2 * d), BF16)
    return y.reshape(b, seq_len, d)


def _ctxconv_kernel(u_ref, kc_ref, cf_ref, ci_ref, o_ref, *, seq_len):
    nf = 2 * seq_len
    kh, kl = _split_bf16(kc_ref[...])
    ksp = _dot(cf_ref[...], kh) + _dot(cf_ref[...], kl)
    kr, ki = ksp[0:nf], ksp[nf:2 * nf]
    for b in range(u_ref.shape[0]):
        x = _dot(cf_ref[:, 0:seq_len], u_ref[b])
        yr, yi = _cmul(x[0:nf], x[nf:2 * nf], kr, ki)
        y = jnp.concatenate([yr, yi], axis=0).astype(BF16)
        o_ref[b] = _dot(ci_ref[...], y).astype(o_ref.dtype)


def _short_seq_conv(u, kc):
    b, seq_len, d = u.shape
    nf = 2 * seq_len
    ph = 2.0 * np.pi * np.outer(np.arange(nf), np.arange(nf)) / nf
    cf = jnp.asarray(np.concatenate([np.cos(ph), -np.sin(ph)], axis=0).astype(np.float32))
    ci = jnp.asarray((np.concatenate([np.cos(ph), -np.sin(ph)], axis=1)[:seq_len] / nf).astype(np.float32))
    return pl.pallas_call(
        functools.partial(_ctxconv_kernel, seq_len=seq_len),
        grid=(1,),
        in_specs=[_full((b, seq_len, d)), _full((nf, d)), _full((2 * nf, nf)), _full((seq_len, 2 * nf))],
        out_specs=_full((b, seq_len, d)),
        out_shape=jax.ShapeDtypeStruct((b, seq_len, d), BF16),
        compiler_params=_params(1),
        name="hyena_ctx_conv",
    )(u, kc, cf.astype(BF16), ci.astype(BF16))


def _hy_outproj_kernel(x_ref, x0_ref, u_ref, yc_ref, mod_ref, skip_ref, w_ref, b_ref, o_ref):
    u = u_ref[...].astype(F32)
    y = x0_ref[...].astype(F32) * (yc_ref[...].astype(F32) + u * skip_ref[...])
    dx = _dot(y.astype(BF16), w_ref[...]) + b_ref[...]
    o_ref[...] = x_ref[...] + mod_ref[2:3, :] * dx


def _hyena_outproj(x2, x0, u, yc, mod_l, mod_index, seq_len, skip, w_bf, b):
    t_rows, d = x2.shape
    tm = min(512, seq_len)
    tps = seq_len // tm
    row = pl.BlockSpec((tm, d), lambda t: (t, 0))
    return pl.pallas_call(
        _hy_outproj_kernel,
        grid=(t_rows // tm,),
        in_specs=[row, row, row, row,
                  pl.BlockSpec((None, MOD_CHUNKS, d), lambda t: (mod_index(t, tps), 0, 0)),
                  _full((1, d)), _full((d, d)), _full((1, d))],
        out_specs=row,
        out_shape=jax.ShapeDtypeStruct((t_rows, d), F32),
        compiler_params=_params(1),
        name="hyena_outproj",
    )(x2, x0, u, yc, mod_l, skip.reshape(1, d), w_bf, b.reshape(1, d))


def _mlp_kernel(x_ref, mod_ref, g_ref, w1_ref, w2_ref, o_ref, *, ff_chunk):
    x = x_ref[...]
    h = _norm_mod(x, g_ref[...], mod_ref[3:4, :], mod_ref[4:5, :]).astype(BF16)
    acc = jnp.zeros(x.shape, F32)
    for c0 in range(0, w1_ref.shape[1], ff_chunk):
        a = jnp.maximum(_dot(h, w1_ref[:, c0:c0 + ff_chunk]), 0.0)
        acc = acc + _dot((a * a).astype(BF16), w2_ref[c0:c0 + ff_chunk, :])
    o_ref[...] = x + mod_ref[5:6, :] * acc


def _mlp(x2, mod_l, mod_index, seq_len, g, w1_bf, w2_bf):
    t_rows, d = x2.shape
    ff = w1_bf.shape[1]
    tm = min(512, seq_len)
    tps = seq_len // tm
    row = pl.BlockSpec((tm, d), lambda t: (t, 0))
    return pl.pallas_call(
        functools.partial(_mlp_kernel, ff_chunk=1024),
        grid=(t_rows // tm,),
        in_specs=[row,
                  pl.BlockSpec((None, MOD_CHUNKS, d), lambda t: (mod_index(t, tps), 0, 0)),
                  _full((1, d)), _full((d, ff)), _full((ff, d))],
        out_specs=row,
        out_shape=jax.ShapeDtypeStruct((t_rows, d), F32),
        compiler_params=_params(1),
        name="mlp",
    )(x2, mod_l, g.reshape(1, d), w1_bf, w2_bf)


def _rope(v, cos, sin_signed, hi_half):
    partner = jnp.where(hi_half, pltpu.roll(v, 16, axis=1), pltpu.roll(v, LANES - 16, axis=1))
    return v * cos + partner * sin_signed


def _qkv_kernel(*refs, with_q, with_rope, q_width, kv_width):
    x_ref, mod_ref, g_ref, w_ref, b_ref, bd_ref, qn_ref, kn_ref = refs[:8]
    idx = 8
    if with_rope:
        cos_ref, sin_ref = refs[idx:idx + 2]
        idx += 2
    outs = refs[idx:]
    if with_q:
        q_ref, k_ref, v_ref = outs
    else:
        k_ref, v_ref = outs
    h = _norm_mod(x_ref[...], g_ref[...], mod_ref[0:1, :], mod_ref[1:2, :]).astype(BF16)
    tm = h.shape[0]
    if with_rope:
        cos = cos_ref[...]
        sin = sin_ref[...]
        hi_half = (lax.broadcasted_iota(jnp.int32, (tm, LANES), 1) & 16) != 0

    def head_norm(p, w, bd):
        ms = _dot((p * p).astype(BF16), bd)
        return p * lax.rsqrt(ms + EPS) * w

    def finish(p, out_ref, scale):
        if scale != 1.0:
            p = p * scale
        if with_rope:
            for c0 in range(0, p.shape[1], LANES):
                out_ref[:, c0:c0 + LANES] = _rope(p[:, c0:c0 + LANES], cos, sin, hi_half).astype(out_ref.dtype)
        else:
            out_ref[...] = p.astype(out_ref.dtype)

    off = 0
    if with_q:
        q = _dot(h, w_ref[:, 0:q_width]) + b_ref[:, 0:q_width]
        finish(head_norm(q, qn_ref[...], bd_ref[...]), q_ref, HEAD_DIM ** -0.5)
        off = q_width
    k = _dot(h, w_ref[:, off:off + kv_width]) + b_ref[:, off:off + kv_width]
    finish(head_norm(k, kn_ref[...], bd_ref[0:kv_width, 0:kv_width]), k_ref, 1.0)
    v = _dot(h, w_ref[:, off + kv_width:off + 2 * kv_width]) + b_ref[:, off + kv_width:off + 2 * kv_width]
    v_ref[...] = v.astype(v_ref.dtype)


def _rope_tables(seq_len):
    rows = seq_len // GRID_W
    row = jnp.repeat(jnp.arange(rows, dtype=F32), GRID_W)
    col = jnp.tile(jnp.arange(GRID_W, dtype=F32), rows)
    inv = ROPE_BASE ** (-jnp.arange(ROPE_NFREQ, dtype=F32) / ROPE_NFREQ)
    ar = row[:, None] * inv[None, :]
    ac = col[:, None] * inv[None, :]
    cos = jnp.concatenate([jnp.cos(ar), jnp.cos(ar), jnp.cos(ac), jnp.cos(ac)], axis=-1)
    sin = jnp.concatenate([-jnp.sin(ar), jnp.sin(ar), -jnp.sin(ac), jnp.sin(ac)], axis=-1)
    return jnp.tile(cos, (1, LANES // HEAD_DIM)), jnp.tile(sin, (1, LANES // HEAD_DIM))


def _qkv_proj(x2, mod_l, mod_index, seq_len, g, w_bf, b, qn, kn, with_q, rope):
    t_rows, d = x2.shape
    q_width = N_HEADS * HEAD_DIM
    kv_width = N_KV_HEADS * HEAD_DIM
    width = w_bf.shape[1]
    tm = min(512, seq_len)
    tps = seq_len // tm
    bd = jnp.asarray(np.kron(np.eye(N_HEADS), np.full((HEAD_DIM, HEAD_DIM), 1.0 / HEAD_DIM)).astype(np.float32))
    row = lambda w: pl.BlockSpec((tm, w), lambda t: (t, 0))
    in_specs = [row(d),
                pl.BlockSpec((None, MOD_CHUNKS, d), lambda t: (mod_index(t, tps), 0, 0)),
                _full((1, d)), _full((d, width)), _full((1, width)), _full((q_width, q_width)),
                _full((1, q_width)), _full((1, kv_width))]
    args = [x2, mod_l, g.reshape(1, d), w_bf, b.reshape(1, width), bd.astype(BF16),
            jnp.tile(qn, N_HEADS).reshape(1, q_width), jnp.tile(kn, N_KV_HEADS).reshape(1, kv_width)]
    if rope is not None:
        in_specs += [pl.BlockSpec((tm, LANES), lambda t: (t % tps, 0))] * 2
        args += list(rope)
    out_specs = [row(kv_width), row(kv_width)]
    out_shape = [jax.ShapeDtypeStruct((t_rows, kv_width), BF16)] * 2
    if with_q:
        out_specs = [row(q_width)] + out_specs
        out_shape = [jax.ShapeDtypeStruct((t_rows, q_width), BF16)] + out_shape
    return pl.pallas_call(
        functools.partial(_qkv_kernel, with_q=with_q, with_rope=rope is not None,
                          q_width=q_width, kv_width=kv_width),
        grid=(t_rows // tm,),
        in_specs=in_specs, out_specs=out_specs, out_shape=out_shape,
        compiler_params=_params(1),
        name="qkv_proj",
    )(*args)


def _attn_kernel(sink_ref, x_ref, q_ref, kp_ref, km_ref, kn_ref, vp_ref, vm_ref, vn_ref, kc_ref, vc_ref,
                 mod_ref, w_ref, b_ref, o_ref, att_ref, *, tq, n_qt):
    i = pl.program_id(1)
    lc = kc_ref.shape[0]
    nk = tq + 2 * WINDOW + lc
    r = lax.broadcasted_iota(jnp.int32, (tq, nk), 0)
    c = lax.broadcasted_iota(jnp.int32, (tq, nk), 1)
    rel = c - WINDOW - r
    ok = jnp.logical_and(rel >= -WINDOW, rel <= WINDOW)
    ok = jnp.logical_and(ok, jnp.logical_or(i > 0, c >= WINDOW))
    ok = jnp.logical_and(ok, jnp.logical_or(i < n_qt - 1, c < tq + WINDOW))
    ok = jnp.logical_or(ok, c >= tq + 2 * WINDOW)
    bias = jnp.where(ok, 0.0, NEG_BIG)
    for hk in range(N_KV_HEADS):
        ls = slice(hk * HEAD_DIM, (hk + 1) * HEAD_DIM)
        kh = jnp.concatenate([kp_ref[:, ls], km_ref[:, ls], kn_ref[:, ls], kc_ref[:, ls]], axis=0)
        vh = jnp.concatenate([vp_ref[:, ls], vm_ref[:, ls], vn_ref[:, ls], vc_ref[:, ls]], axis=0)
        for gq in range(GROUP):
            hd = hk * GROUP + gq
            qh = q_ref[:, hd * HEAD_DIM:(hd + 1) * HEAD_DIM]
            s = lax.dot_general(qh, kh, (((1,), (1,)), ((), ())), preferred_element_type=F32) + bias
            sink = sink_ref[hd]
            m = jnp.maximum(jnp.max(s, axis=-1, keepdims=True), sink)
            p = jnp.exp(s - m)
            den = jnp.sum(p, axis=-1, keepdims=True) + jnp.exp(sink - m)
            o = _dot(p.astype(BF16), vh) / den
            att_ref[:, hd * HEAD_DIM:(hd + 1) * HEAD_DIM] = o.astype(BF16)
    dx = _dot(att_ref[...], w_ref[...]) + b_ref[...]
    o_ref[...] = x_ref[...] + mod_ref[2:3, :] * dx


def _window_attention(x2, q, k, v, kc, vc, sink, mod_l, batch, seq_len, w_bf, b):
    t_rows, d = x2.shape
    lc = kc.shape[0] // batch
    kvw = k.shape[1]
    tq = min(256, seq_len)
    n_qt = seq_len // tq
    wpt = tq // WINDOW
    nwb = t_rows // WINDOW
    qrow = lambda w: pl.BlockSpec((tq, w), lambda bi, i: (bi * n_qt + i, 0))
    prev = pl.BlockSpec((WINDOW, kvw), lambda bi, i: (jnp.maximum((bi * n_qt + i) * wpt - 1, 0), 0))
    nxt = pl.BlockSpec((WINDOW, kvw), lambda bi, i: (jnp.minimum((bi * n_qt + i + 1) * wpt, nwb - 1), 0))
    ctxb = pl.BlockSpec((lc, kvw), lambda bi, i: (bi, 0))
    return pl.pallas_call(
        functools.partial(_attn_kernel, tq=tq, n_qt=n_qt),
        grid=(batch, n_qt),
        in_specs=[pl.BlockSpec(memory_space=pltpu.SMEM),
                  qrow(d), qrow(q.shape[1]),
                  prev, qrow(kvw), nxt, prev, qrow(kvw), nxt, ctxb, ctxb,
                  pl.BlockSpec((None, MOD_CHUNKS, d), lambda bi, i: (bi, 0, 0)),
                  pl.BlockSpec((d, d), lambda bi, i: (0, 0)), pl.BlockSpec((1, d), lambda bi, i: (0, 0))],
        out_specs=qrow(d),
        out_shape=jax.ShapeDtypeStruct((t_rows, d), F32),
        scratch_shapes=[pltpu.VMEM((tq, d), BF16)],
        compiler_params=_params(2),
        name="window_attention",
    )(sink, x2, q, k, k, k, v, v, v, kc, vc, mod_l, w_bf, b.reshape(1, d))


def _latent_mod_index(t, tiles_per_seq):
    return t // tiles_per_seq


def kernel(x, c, ctx, c_ctx, mod_w, mod_b, norm1_w, norm2_w, mlp_w1, mlp_w2, hy_w_in, hy_b_in, hy_conv_w,
           hy_conv_b, hy_f_w1, hy_f_b1, hy_f_freq1, hy_f_w2, hy_f_b2, hy_f_freq2, hy_f_w3, hy_skip, hy_w_out,
           hy_b_out, at_w_qkv, at_b_qkv, at_q_norm, at_k_norm, at_sink, at_w_out, at_b_out):
    batch, seq_len, d = x.shape
    lc = ctx.shape[1]
    ctx_row = batch

    def ctx_mod_index(t, tiles_per_seq):
        return ctx_row

    cond_rows = -(-(batch + 1) // 8) * 8
    cc = jnp.zeros((cond_rows, d), F32).at[:batch].set(c).at[batch].set(c_ctx)
    mod = _modulation(cc, mod_w, mod_b)

    x2 = x.reshape(batch * seq_len, d)
    c2 = ctx.reshape(batch * lc, d)

    w_in = hy_w_in[0].astype(BF16)
    w_out = hy_w_out[0].astype(BF16)
    w1 = mlp_w1[0].astype(BF16)
    w2 = mlp_w2[0].astype(BF16)
    fpar = (hy_f_w1[0], hy_f_b1[0], hy_f_freq1[0], hy_f_w2[0], hy_f_b2[0], hy_f_freq2[0], hy_f_w3[0])
    def hyena_layer(tok, n_seq, idx_fn, conv):
        x0, u = _hyena_inproj(tok, mod[0], idx_fn, n_seq, norm1_w[0], w_in, hy_b_in[0], hy_conv_w[0],
                              hy_conv_b[0])
        kc = _hyena_filter(n_seq, d, *fpar)
        yc = conv(u.reshape(batch, n_seq, d), kc).reshape(batch * n_seq, d)
        tok = _hyena_outproj(tok, x0, u, yc, mod[0], idx_fn, n_seq, hy_skip[0], w_out, hy_b_out[0])
        return _mlp(tok, mod[0], idx_fn, n_seq, norm2_w[0], w1, w2)

    x2 = hyena_layer(x2, seq_len, _latent_mod_index, _long_conv)
    c2 = hyena_layer(c2, lc, ctx_mod_index, _short_seq_conv)

    w_qkv = at_w_qkv[0].astype(BF16)
    q_width = N_HEADS * HEAD_DIM
    rope = _rope_tables(seq_len)
    q, k, v = _qkv_proj(x2, mod[1], _latent_mod_index, seq_len, norm1_w[1], w_qkv, at_b_qkv[0],
                        at_q_norm[0], at_k_norm[0], True, rope)
    kctx, vctx = _qkv_proj(c2, mod[1], ctx_mod_index, lc, norm1_w[1], w_qkv[:, q_width:],
                           at_b_qkv[0][q_width:], at_q_norm[0], at_k_norm[0], False, None)
    x2 = _window_attention(x2, q, k, v, kctx, vctx, at_sink[0].astype(F32), mod[1], batch, seq_len,
                           at_w_out[0].astype(BF16), at_b_out[0])
    x2 = _mlp(x2, mod[1], _latent_mod_index, seq_len, norm2_w[1], mlp_w1[1].astype(BF16),
              mlp_w2[1].astype(BF16))
    return x2.reshape(batch, seq_len, d)
```

```python
import functools
import math

import numpy as np
import jax
import jax.numpy as jnp
from jax import lax
from jax.experimental import pallas as pl
from jax.experimental.pallas import tpu as pltpu

F32 = jnp.float32
BF16 = jnp.bfloat16

EPS = 1e-6
MOD_CHUNKS = 6
GRID_W = 64
HY_SHORT = 3
HY_BANDS = 16
HY_DECAY_TARGET = 1e-2
HY_MAX_DECAY = math.log(HY_DECAY_TARGET) / 0.3
HY_MIN_DECAY = math.log(HY_DECAY_TARGET) / 1.5
N_HEADS = 16
N_KV_HEADS = 4
GROUP = N_HEADS // N_KV_HEADS
HEAD_DIM = 64
WINDOW = 128
ROPE_BASE = 10000.0
ROPE_NFREQ = HEAD_DIM // 4
LOG2_E = math.log2(math.e)
Q_SCALE_LOG2 = HEAD_DIM ** -0.5 * LOG2_E

V7X_VMEM_BYTES = 64 * 1024 * 1024
VMEM_LIMIT = V7X_VMEM_BYTES - 8 * 1024 * 1024
LANES = 128
BF16_SUBLANES = 16
FFT_N2 = 128
NEG_BIG = -1e30


def _params(n_axes):
    return pltpu.CompilerParams(dimension_semantics=("arbitrary",) * n_axes,
                                vmem_limit_bytes=VMEM_LIMIT)


def _split_bf16(a):
    hi = a.astype(BF16)
    lo = (a - hi.astype(F32)).astype(BF16)
    return hi, lo


def _dot(a, b):
    return jnp.dot(a, b, preferred_element_type=F32)


def _dot3(a, b):
    ah, al = _split_bf16(a)
    bh, bl = _split_bf16(b)
    return _dot(ah, bh) + (_dot(ah, bl) + _dot(al, bh))


def _norm_mod(x, g, shift, scale):
    ms = jnp.mean(x * x, axis=-1, keepdims=True)
    y = x * lax.rsqrt(ms + EPS) * g
    return y * (1.0 + scale) + shift


def _full(shape):
    nd = len(shape)
    return pl.BlockSpec(shape, lambda *_: (0,) * nd)


def _mod_kernel(c_ref, w_ref, b_ref, o_ref):
    c = c_ref[...]
    s = c * jax.nn.sigmoid(c)
    o_ref[...] = _dot3(s, w_ref[...]) + b_ref[...]


def _modulation(cc, mod_w, mod_b):
    depth, d, n = mod_w.shape
    r = cc.shape[0]
    tn = 1536
    out = pl.pallas_call(
        _mod_kernel,
        grid=(depth, n // tn),
        in_specs=[pl.BlockSpec((r, d), lambda i, j: (0, 0)),
                  pl.BlockSpec((None, d, tn), lambda i, j: (i, 0, j)),
                  pl.BlockSpec((None, 1, tn), lambda i, j: (i, 0, j))],
        out_specs=pl.BlockSpec((None, r, tn), lambda i, j: (i, 0, j)),
        out_shape=jax.ShapeDtypeStruct((depth, r, n), F32),
        compiler_params=_params(2),
        name="modulation",
    )(cc, mod_w, mod_b.reshape(depth, 1, n))
    return out.reshape(depth, r, MOD_CHUNKS, d)


def _inproj_kernel(x_ref, xp_ref, xn_ref, mod_ref, g_ref, w_ref, b_ref, cw_ref, cb_ref,
                   x0_ref, u_ref, p0_ref, p1_ref, p2_ref, *, tm, tiles_per_seq, halo, cw_cols):
    t = pl.program_id(0)
    first = (t % tiles_per_seq) == 0
    last = (t % tiles_per_seq) == tiles_per_seq - 1
    g = g_ref[...]
    shift = mod_ref[0:1, :]
    scale = mod_ref[1:2, :]

    def nm(v):
        return _norm_mod(v, g, shift, scale).astype(BF16)

    h = jnp.concatenate([nm(xp_ref[...]), nm(x_ref[...]), nm(xn_ref[...])], axis=0)
    d = x_ref.shape[1]
    row = lax.broadcasted_iota(jnp.int32, (halo, cw_cols), 0)
    kill_top = jnp.logical_and(first, row == halo - 1)
    kill_bot = jnp.logical_and(last, row == 0)

    def conv(col0, p_ref):
        p = _dot(h, w_ref[:, col0:col0 + cw_cols]) + b_ref[:, col0:col0 + cw_cols]
        p_ref[0:halo, :] = jnp.where(kill_top, 0.0, p[0:halo])
        p_ref[halo:tm + halo, :] = p[halo:tm + halo]
        p_ref[tm + halo:tm + 2 * halo, :] = jnp.where(kill_bot, 0.0, p[tm + halo:tm + 2 * halo])
        out = cb_ref[:, col0:col0 + cw_cols]
        for k in range(HY_SHORT):
            out = out + p_ref[halo - 1 + k:halo - 1 + k + tm, :] * cw_ref[k:k + 1, col0:col0 + cw_cols]
        return out

    for c0 in range(0, d, cw_cols):
        x0_ref[:, c0:c0 + cw_cols] = conv(c0, p0_ref).astype(BF16)
        u_ref[:, c0:c0 + cw_cols] = (conv(d + c0, p1_ref) * conv(2 * d + c0, p2_ref)).astype(BF16)


def _hyena_inproj(x2, mod_l, mod_index, seq_len, g, w_bf, b, conv_w, conv_b):
    t_rows, d = x2.shape
    tm = min(512, seq_len)
    halo = 16
    tps = seq_len // tm
    nblk = t_rows // halo
    cw_cols = 256
    kern = functools.partial(_inproj_kernel, tm=tm, tiles_per_seq=tps, halo=halo, cw_cols=cw_cols)
    return pl.pallas_call(
        kern,
        grid=(t_rows // tm,),
        in_specs=[pl.BlockSpec((tm, d), lambda t: (t, 0)),
                  pl.BlockSpec((halo, d), lambda t: (jnp.maximum(t * (tm // halo) - 1, 0), 0)),
                  pl.BlockSpec((halo, d), lambda t: (jnp.minimum((t + 1) * (tm // halo), nblk - 1), 0)),
                  pl.BlockSpec((None, MOD_CHUNKS, d), lambda t: (mod_index(t, tps), 0, 0)),
                  _full((1, d)), _full((d, 3 * d)), _full((1, 3 * d)),
                  _full((HY_SHORT, 3 * d)), _full((1, 3 * d))],
        out_specs=[pl.BlockSpec((tm, d), lambda t: (t, 0)),
                   pl.BlockSpec((tm, d), lambda t: (t, 0))],
        out_shape=[jax.ShapeDtypeStruct((t_rows, d), BF16),
                   jax.ShapeDtypeStruct((t_rows, d), BF16)],
        scratch_shapes=[pltpu.VMEM((tm + 2 * halo, cw_cols), F32)] * 3,
        compiler_params=_params(1),
        name="hyena_inproj",
    )(x2, x2, x2, mod_l, g.reshape(1, d), w_bf, b.reshape(1, 3 * d), conv_w, conv_b.reshape(1, 3 * d))


def _filter_hidden_kernel(z_ref, w1_ref, b1_ref, f1_ref, w2_ref, b2_ref, f2_ref, o_ref):
    h = jnp.sin(f1_ref[...] * (_dot3(z_ref[...], w1_ref[...]) + b1_ref[...]))
    o_ref[...] = jnp.sin(f2_ref[...] * (_dot3(h, w2_ref[...]) + b2_ref[...]))


def _filter_out_kernel(h_ref, w3_ref, dl_ref, o_ref, sum_ref, *, seq_len):
    phase = pl.program_id(1)
    i = pl.program_id(2)
    tr, td = o_ref.shape
    r = lax.broadcasted_iota(jnp.int32, (tr, td), 0) + i * tr
    pos = jnp.where(r < seq_len, r, 2 * seq_len - r)
    t = pos.astype(F32) / seq_len
    kc = _dot3(h_ref[...], w3_ref[...]) * jnp.exp(-t * dl_ref[...])
    kc = jnp.where(r == seq_len, 0.0, kc)

    @pl.when(jnp.logical_and(phase == 0, i == 0))
    def _():
        sum_ref[...] = jnp.zeros_like(sum_ref)

    @pl.when(phase == 0)
    def _():
        sum_ref[...] += jnp.sum(jnp.abs(kc), axis=0, keepdims=True)

    @pl.when(phase == 1)
    def _():
        o_ref[...] = kc / sum_ref[...]


def _filter_features(seq_len):
    f32 = jnp.float32
    pos_f = jnp.arange(seq_len, dtype=f32)
    pos_b = ((seq_len - jnp.arange(seq_len)) % seq_len).astype(f32)
    pos = jnp.concatenate([pos_f, pos_b])
    t = pos / seq_len
    bands = jnp.linspace(1e-4, HY_BANDS - 1, HY_BANDS, dtype=f32)
    ang = (2.0 * math.pi / seq_len) * pos[:, None] * bands[None, :]
    return jnp.concatenate([t[:, None], jnp.cos(ang), -jnp.sin(ang)], axis=-1)


def _hyena_filter(seq_len, d, f_w1, f_b1, f_fr1, f_w2, f_b2, f_fr2, f_w3):
    z = _filter_features(seq_len)
    emb = z.shape[1]
    hid = f_w1.shape[1]
    rows = 2 * seq_len
    tr = min(2048, seq_len)
    h2 = pl.pallas_call(
        _filter_hidden_kernel,
        grid=(rows // tr,),
        in_specs=[pl.BlockSpec((tr, emb), lambda i: (i, 0)),
                  _full((emb, hid)), _full((1, hid)), _full((1, hid)),
                  _full((hid, hid)), _full((1, hid)), _full((1, hid))],
        out_specs=pl.BlockSpec((tr, hid), lambda i: (i, 0)),
        out_shape=jax.ShapeDtypeStruct((rows, hid), F32),
        compiler_params=_params(1),
        name="hyena_filter_hidden",
    )(z, f_w1, f_b1.reshape(1, hid), f_fr1.reshape(1, hid), f_w2, f_b2.reshape(1, hid),
      f_fr2.reshape(1, hid))
    deltas = jnp.abs(jnp.linspace(HY_MIN_DECAY, HY_MAX_DECAY, d, dtype=F32)).reshape(1, d)
    td = 256
    nd = d // td
    tiles_per_half = seq_len // tr
    return pl.pallas_call(
        functools.partial(_filter_out_kernel, seq_len=seq_len),
        grid=(nd, 2, rows // tr),
        in_specs=[pl.BlockSpec((tr, hid), lambda j, p, i: (i, 0)),
                  pl.BlockSpec((hid, td), lambda j, p, i: (0, (i // tiles_per_half) * nd + j)),
                  pl.BlockSpec((1, td), lambda j, p, i: (0, j))],
        out_specs=pl.BlockSpec((tr, td), lambda j, p, i: (p * i, j)),
        out_shape=jax.ShapeDtypeStruct((rows, d), F32),
        scratch_shapes=[pltpu.VMEM((1, td), F32)],
        compiler_params=_params(3),
        name="hyena_filter_out",
    )(h2, f_w3, deltas)


def _dft_tables(n1):
    n2 = FFT_N2
    n = n1 * n2
    h1 = n1 // 2
    k1 = np.concatenate([np.arange(0, h1 + 1), np.arange(1, h1)])
    is_im = np.concatenate([np.zeros(h1 + 1, bool), np.ones(h1 - 1, bool)])
    th = 2.0 * np.pi * np.outer(k1, np.arange(n1)) / n1
    f1 = np.where(is_im[:, None], -np.sin(th), np.cos(th))
    wgt = np.where((k1 == 0) | (k1 == h1), 1.0, 2.0)[:, None] / n
    f1inv = (np.where(is_im[:, None], -np.sin(th), np.cos(th)) * wgt).T
    m = np.arange(n2)
    k2 = np.arange(n2)

    def g(k1v):
        ph = 2.0 * np.pi * np.outer(k2 * n1 + k1v, m) / n
        return np.cos(ph), -np.sin(ph)

    mf = np.zeros((h1, 2 * n2, 2 * n2))
    for j in range(1, h1):
        gr, gi = g(j)
        mf[j] = np.block([[gr, -gi], [gi, gr]])
    g0r, g0i = g(0)
    ghr, ghi = g(h1)
    zz = np.zeros((n2, n2))
    m0 = np.block([[g0r, zz], [g0i, zz], [zz, ghr], [zz, ghi]])
    mfi = np.transpose(mf, (0, 2, 1))
    m0i = m0.T
    as32 = lambda a: jnp.asarray(a.astype(np.float32))
    return dict(f1=as32(f1), f1inv=as32(f1inv), mf=as32(mf), mfi=as32(mfi), m0=as32(m0), m0i=as32(m0i))


def _stage1_kernel(f_ref, x_ref, o_ref):
    xt = pltpu.einshape("ntd->tnd", x_ref[...].astype(BF16))
    f = f_ref[...]
    out = jnp.stack([_dot(f, xt[t]).astype(o_ref.dtype) for t in range(xt.shape[0])], axis=0)
    o_ref[...] = pltpu.einshape("tmd->mtd", out)


def _dft_stage1(f_bf, x4, out_dtype):
    bx, r, n2, d = x4.shape
    m = f_bf.shape[0]
    tt = BF16_SUBLANES
    return pl.pallas_call(
        _stage1_kernel,
        grid=(bx, n2 // tt),
        in_specs=[_full((m, r)), pl.BlockSpec((None, r, tt, d), lambda b, j: (b, 0, j, 0))],
        out_specs=pl.BlockSpec((None, m, tt, d), lambda b, j: (b, 0, j, 0)),
        out_shape=jax.ShapeDtypeStruct((bx, m, n2, d), out_dtype),
        compiler_params=_params(2),
        name="dft_stage1",
    )(f_bf, x4)


def _filter_spec_kernel(a_ref, mf_ref, m0_ref, ks_ref, kh_ref):
    j = pl.program_id(0)
    n2 = FFT_N2
    a = a_ref[...].reshape(2 * n2, a_ref.shape[-1])

    @pl.when(j == 0)
    def _():
        x = _dot(m0_ref[...], a)
        ks_ref[...] = x[0:2 * n2].reshape(ks_ref.shape)
        kh_ref[...] = x[2 * n2:4 * n2].reshape(kh_ref.shape)

    @pl.when(j > 0)
    def _():
        ks_ref[...] = _dot(mf_ref[...], a).reshape(ks_ref.shape)


def _filter_spectrum(ak, tabs):
    _, h1, n2, d = ak.shape
    return pl.pallas_call(
        _filter_spec_kernel,
        grid=(h1,),
        in_specs=[pl.BlockSpec((2, None, n2, d), lambda j: (0, j, 0, 0)),
                  pl.BlockSpec((None, 2 * n2, 2 * n2), lambda j: (j, 0, 0)),
                  _full((4 * n2, 2 * n2))],
        out_specs=[pl.BlockSpec((None, 2, n2, d), lambda j: (j, 0, 0, 0)),
                   _full((2, n2, d))],
        out_shape=[jax.ShapeDtypeStruct((h1, 2, n2, d), F32),
                   jax.ShapeDtypeStruct((2, n2, d), F32)],
        compiler_params=_params(1),
        name="hyena_filter_spectrum",
    )(ak, tabs["mf"].astype(BF16), tabs["m0"].astype(BF16))


def _cmul(xr, xi, kr, ki):
    return xr * kr - xi * ki, xr * ki + xi * kr


def _stage2_kernel(a_ref, mf_ref, mfi_ref, m0_ref, m0i_ref, ks_ref, kh_ref, o_ref):
    j = pl.program_id(0)
    n2 = FFT_N2
    nb = a_ref.shape[0]
    d = a_ref.shape[-1]

    @pl.when(j == 0)
    def _():
        for b in range(nb):
            a = a_ref[b].reshape(2 * n2, d)
            x = _dot(m0_ref[...], a)
            y0r, y0i = _cmul(x[0:n2], x[n2:2 * n2], ks_ref[0], ks_ref[1])
            y1r, y1i = _cmul(x[2 * n2:3 * n2], x[3 * n2:4 * n2], kh_ref[0], kh_ref[1])
            y = jnp.concatenate([y0r, y0i, y1r, y1i], axis=0).astype(BF16)
            o_ref[b] = _dot(m0i_ref[...], y).reshape(2, n2, d).astype(o_ref.dtype)

    @pl.when(j > 0)
    def _():
        for b in range(nb):
            a = a_ref[b].reshape(2 * n2, d)
            x = _dot(mf_ref[...], a)
            yr, yi = _cmul(x[0:n2], x[n2:2 * n2], ks_ref[0], ks_ref[1])
            y = jnp.concatenate([yr, yi], axis=0).astype(BF16)
            o_ref[b] = _dot(mfi_ref[...], y).reshape(2, n2, d).astype(o_ref.dtype)


def _dft_stage2(a5, tabs, ks, kh):
    nb, _, h1, n2, d = a5.shape
    blk = pl.BlockSpec((nb, 2, None, n2, d), lambda j: (0, 0, j, 0, 0))
    return pl.pallas_call(
        _stage2_kernel,
        grid=(h1,),
        in_specs=[blk,
                  pl.BlockSpec((None, 2 * n2, 2 * n2), lambda j: (j, 0, 0)),
                  pl.BlockSpec((None, 2 * n2, 2 * n2), lambda j: (j, 0, 0)),
                  _full((4 * n2, 2 * n2)), _full((2 * n2, 4 * n2)),
                  pl.BlockSpec((None, 2, n2, d), lambda j: (j, 0, 0, 0)),
                  _full((2, n2, d))],
        out_specs=blk,
        out_shape=jax.ShapeDtypeStruct(a5.shape, BF16),
        compiler_params=_params(1),
        name="dft_stage2",
    )(a5, tabs["mf"].astype(BF16), tabs["mfi"].astype(BF16), tabs["m0"].astype(BF16),
      tabs["m0i"].astype(BF16), ks, kh)


def _long_conv(u, kc):
    b, seq_len, d = u.shape
    n2 = FFT_N2
    n1 = 2 * seq_len // n2
    h1 = n1 // 2
    tabs = _dft_tables(n1)
    f1 = tabs["f1"].astype(BF16)
    ak = _dft_stage1(f1, kc.reshape(1, n1, n2, d), BF16)
    ks, kh = _filter_spectrum(ak.reshape(2, h1, n2, d), tabs)
    a = _dft_stage1(f1[:, :h1], u.reshape(b, h1, n2, d), BF16)
    bv = _dft_stage2(a.reshape(b, 2, h1, n2, d), tabs, ks, kh)
    y = _dft_stage1(tabs["f1inv"][:h1].astype(BF16), bv.reshape(b, n1, n2, d), BF16)
    return y.reshape(b, seq_len, d)


def _ctxconv_kernel(u_ref, kc_ref, cf_ref, ci_ref, o_ref, *, seq_len):
    nf = 2 * seq_len
    kh, kl = _split_bf16(kc_ref[...])
    ksp = _dot(cf_ref[...], kh) + _dot(cf_ref[...], kl)
    kr, ki = ksp[0:nf], ksp[nf:2 * nf]
    for b in range(u_ref.shape[0]):
        x = _dot(cf_ref[:, 0:seq_len], u_ref[b])
        yr, yi = _cmul(x[0:nf], x[nf:2 * nf], kr, ki)
        y = jnp.concatenate([yr, yi], axis=0).astype(BF16)
        o_ref[b] = _dot(ci_ref[...], y).astype(o_ref.dtype)


def _short_seq_conv(u, kc):
    b, seq_len, d = u.shape
    nf = 2 * seq_len
    ph = 2.0 * np.pi * np.outer(np.arange(nf), np.arange(nf)) / nf
    cf = jnp.asarray(np.concatenate([np.cos(ph), -np.sin(ph)], axis=0).astype(np.float32))
    ci = jnp.asarray((np.concatenate([np.cos(ph), -np.sin(ph)], axis=1)[:seq_len] / nf).astype(np.float32))
    return pl.pallas_call(
        functools.partial(_ctxconv_kernel, seq_len=seq_len),
        grid=(1,),
        in_specs=[_full((b, seq_len, d)), _full((nf, d)), _full((2 * nf, nf)), _full((seq_len, 2 * nf))],
        out_specs=_full((b, seq_len, d)),
        out_shape=jax.ShapeDtypeStruct((b, seq_len, d), BF16),
        compiler_params=_params(1),
        name="hyena_ctx_conv",
    )(u, kc, cf.astype(BF16), ci.astype(BF16))


def _hy_outproj_kernel(x_ref, x0_ref, u_ref, yc_ref, mod_ref, skip_ref, w_ref, b_ref, o_ref):
    u = u_ref[...].astype(F32)
    y = x0_ref[...].astype(F32) * (yc_ref[...].astype(F32) + u * skip_ref[...])
    dx = _dot(y.astype(BF16), w_ref[...]) + b_ref[...]
    o_ref[...] = x_ref[...] + mod_ref[2:3, :] * dx


def _hyena_outproj(x2, x0, u, yc, mod_l, mod_index, seq_len, skip, w_bf, b):
    t_rows, d = x2.shape
    tm = min(512, seq_len)
    tps = seq_len // tm
    row = pl.BlockSpec((tm, d), lambda t: (t, 0))
    return pl.pallas_call(
        _hy_outproj_kernel,
        grid=(t_rows // tm,),
        in_specs=[row, row, row, row,
                  pl.BlockSpec((None, MOD_CHUNKS, d), lambda t: (mod_index(t, tps), 0, 0)),
                  _full((1, d)), _full((d, d)), _full((1, d))],
        out_specs=row,
        out_shape=jax.ShapeDtypeStruct((t_rows, d), F32),
        compiler_params=_params(1),
        name="hyena_outproj",
    )(x2, x0, u, yc, mod_l, skip.reshape(1, d), w_bf, b.reshape(1, d))


def _mlp_kernel(x_ref, mod_ref, g_ref, w1_ref, w2_ref, o_ref, *, ff_chunk):
    x = x_ref[...]
    h = _norm_mod(x, g_ref[...], mod_ref[3:4, :], mod_ref[4:5, :]).astype(BF16)
    acc = jnp.zeros(x.shape, F32)
    for c0 in range(0, w1_ref.shape[1], ff_chunk):
        a = jnp.maximum(_dot(h, w1_ref[:, c0:c0 + ff_chunk]), 0.0)
        acc = acc + _dot((a * a).astype(BF16), w2_ref[c0:c0 + ff_chunk, :])
    o_ref[...] = x + mod_ref[5:6, :] * acc


def _mlp(x2, mod_l, mod_index, seq_len, g, w1_bf, w2_bf):
    t_rows, d = x2.shape
    ff = w1_bf.shape[1]
    tm = min(512, seq_len)
    tps = seq_len // tm
    row = pl.BlockSpec((tm, d), lambda t: (t, 0))
    return pl.pallas_call(
        functools.partial(_mlp_kernel, ff_chunk=1024),
        grid=(t_rows // tm,),
        in_specs=[row,
                  pl.BlockSpec((None, MOD_CHUNKS, d), lambda t: (mod_index(t, tps), 0, 0)),
                  _full((1, d)), _full((d, ff)), _full((ff, d))],
        out_specs=row,
        out_shape=jax.ShapeDtypeStruct((t_rows, d), F32),
        compiler_params=_params(1),
        name="mlp",
    )(x2, mod_l, g.reshape(1, d), w1_bf, w2_bf)


def _rope(v, cos, sin_signed, hi_half):
    partner = jnp.where(hi_half, pltpu.roll(v, 16, axis=1), pltpu.roll(v, LANES - 16, axis=1))
    return v * cos + partner * sin_signed


def _qkv_kernel(*refs, with_q, with_rope, q_width, kv_width):
    x_ref, mod_ref, g_ref, w_ref, b_ref, bd_ref, qn_ref, kn_ref = refs[:8]
    idx = 8
    if with_rope:
        cos_ref, sin_ref = refs[idx:idx + 2]
        idx += 2
    outs = refs[idx:]
    if with_q:
        q_ref, k_ref, v_ref = outs
    else:
        k_ref, v_ref = outs
    h = _norm_mod(x_ref[...], g_ref[...], mod_ref[0:1, :], mod_ref[1:2, :]).astype(BF16)
    tm = h.shape[0]
    if with_rope:
        cos = cos_ref[...]
        sin = sin_ref[...]
        hi_half = (lax.broadcasted_iota(jnp.int32, (tm, LANES), 1) & 16) != 0

    def head_norm(p, w, bd):
        ms = _dot((p * p).astype(BF16), bd)
        return p * lax.rsqrt(ms + EPS) * w

    def finish(p, out_ref, scale):
        if scale != 1.0:
            p = p * scale
        if with_rope:
            for c0 in range(0, p.shape[1], LANES):
                out_ref[:, c0:c0 + LANES] = _rope(p[:, c0:c0 + LANES], cos, sin, hi_half).astype(out_ref.dtype)
        else:
            out_ref[...] = p.astype(out_ref.dtype)

    off = 0
    if with_q:
        q = _dot(h, w_ref[:, 0:q_width]) + b_ref[:, 0:q_width]
        finish(head_norm(q, qn_ref[...], bd_ref[...]), q_ref, Q_SCALE_LOG2)
        off = q_width
    k = _dot(h, w_ref[:, off:off + kv_width]) + b_ref[:, off:off + kv_width]
    finish(head_norm(k, kn_ref[...], bd_ref[0:kv_width, 0:kv_width]), k_ref, 1.0)
    v = _dot(h, w_ref[:, off + kv_width:off + 2 * kv_width]) + b_ref[:, off + kv_width:off + 2 * kv_width]
    v_ref[...] = v.astype(v_ref.dtype)


def _rope_tables(seq_len):
    rows = seq_len // GRID_W
    row = jnp.repeat(jnp.arange(rows, dtype=F32), GRID_W)
    col = jnp.tile(jnp.arange(GRID_W, dtype=F32), rows)
    inv = ROPE_BASE ** (-jnp.arange(ROPE_NFREQ, dtype=F32) / ROPE_NFREQ)
    ar = row[:, None] * inv[None, :]
    ac = col[:, None] * inv[None, :]
    cos = jnp.concatenate([jnp.cos(ar), jnp.cos(ar), jnp.cos(ac), jnp.cos(ac)], axis=-1)
    sin = jnp.concatenate([-jnp.sin(ar), jnp.sin(ar), -jnp.sin(ac), jnp.sin(ac)], axis=-1)
    return jnp.tile(cos, (1, LANES // HEAD_DIM)), jnp.tile(sin, (1, LANES // HEAD_DIM))


def _qkv_proj(x2, mod_l, mod_index, seq_len, g, w_bf, b, qn, kn, with_q, rope):
    t_rows, d = x2.shape
    q_width = N_HEADS * HEAD_DIM
    kv_width = N_KV_HEADS * HEAD_DIM
    width = w_bf.shape[1]
    tm = min(512, seq_len)
    tps = seq_len // tm
    bd = jnp.asarray(np.kron(np.eye(N_HEADS), np.full((HEAD_DIM, HEAD_DIM), 1.0 / HEAD_DIM)).astype(np.float32))
    row = lambda w: pl.BlockSpec((tm, w), lambda t: (t, 0))
    in_specs = [row(d),
                pl.BlockSpec((None, MOD_CHUNKS, d), lambda t: (mod_index(t, tps), 0, 0)),
                _full((1, d)), _full((d, width)), _full((1, width)), _full((q_width, q_width)),
                _full((1, q_width)), _full((1, kv_width))]
    args = [x2, mod_l, g.reshape(1, d), w_bf, b.reshape(1, width), bd.astype(BF16),
            jnp.tile(qn, N_HEADS).reshape(1, q_width), jnp.tile(kn, N_KV_HEADS).reshape(1, kv_width)]
    if rope is not None:
        in_specs += [pl.BlockSpec((tm, LANES), lambda t: (t % tps, 0))] * 2
        args += list(rope)
    out_specs = [row(kv_width), row(kv_width)]
    out_shape = [jax.ShapeDtypeStruct((t_rows, kv_width), BF16)] * 2
    if with_q:
        out_specs = [row(q_width)] + out_specs
        out_shape = [jax.ShapeDtypeStruct((t_rows, q_width), BF16)] + out_shape
    return pl.pallas_call(
        functools.partial(_qkv_kernel, with_q=with_q, with_rope=rope is not None,
                          q_width=q_width, kv_width=kv_width),
        grid=(t_rows // tm,),
        in_specs=in_specs, out_specs=out_specs, out_shape=out_shape,
        compiler_params=_params(1),
        name="qkv_proj",
    )(*args)


def _attn_kernel(sink_ref, x_ref, q_ref, kp_ref, km_ref, kn_ref, vp_ref, vm_ref, vn_ref, kc_ref, vc_ref,
                 mod_ref, w_ref, b_ref, o_ref, att_ref, *, tq, n_qt):
    i = pl.program_id(1)
    lc = kc_ref.shape[0]
    nw = 3 * WINDOW
    nk = nw + lc
    n_rb = tq // WINDOW
    r = lax.broadcasted_iota(jnp.int32, (WINDOW, nk), 0)
    c = lax.broadcasted_iota(jnp.int32, (WINDOW, nk), 1)
    rel = c - WINDOW - r
    band = jnp.logical_and(rel >= -WINDOW, rel <= WINDOW)
    biases = []
    for rb in range(n_rb):
        ok = band
        if rb == 0:
            ok = jnp.logical_and(ok, jnp.logical_or(i > 0, c >= WINDOW))
        if rb == n_rb - 1:
            ok = jnp.logical_and(ok, jnp.logical_or(i < n_qt - 1, c < 2 * WINDOW))
        ok = jnp.logical_or(ok, c >= nw)
        biases.append(jnp.where(ok, 0.0, NEG_BIG))
    for hk in range(N_KV_HEADS):
        ls = slice(hk * HEAD_DIM, (hk + 1) * HEAD_DIM)
        kw = jnp.concatenate([kp_ref[:, ls], km_ref[:, ls], kn_ref[:, ls]], axis=0)
        vw = jnp.concatenate([vp_ref[:, ls], vm_ref[:, ls], vn_ref[:, ls]], axis=0)
        kcx = kc_ref[:, ls]
        vcx = vc_ref[:, ls]
        heads = [hk * GROUP + gq for gq in range(GROUP)]
        for rb in range(n_rb):
            rows = slice(rb * WINDOW, (rb + 1) * WINDOW)
            kk = jnp.concatenate([kw[rb * WINDOW:rb * WINDOW + nw], kcx], axis=0)
            vv = jnp.concatenate([vw[rb * WINDOW:rb * WINDOW + nw], vcx], axis=0)
            ss = [lax.dot_general(q_ref[rows, hd * HEAD_DIM:(hd + 1) * HEAD_DIM], kk, (((1,), (1,)), ((), ())),
                                  preferred_element_type=F32) + biases[rb] for hd in heads]
            ms = [jnp.maximum(jnp.max(s, axis=-1, keepdims=True), sink_ref[hd]) for s, hd in zip(ss, heads)]
            ps = [jnp.exp2(s - m) for s, m in zip(ss, ms)]
            dens = [jnp.sum(p, axis=-1, keepdims=True) + jnp.exp2(sink_ref[hd] - m)
                    for p, m, hd in zip(ps, ms, heads)]
            for p, den, hd in zip(ps, dens, heads):
                o = _dot(p.astype(BF16), vv) / den
                att_ref[rows, hd * HEAD_DIM:(hd + 1) * HEAD_DIM] = o.astype(BF16)
    dx = _dot(att_ref[...], w_ref[...]) + b_ref[...]
    o_ref[...] = x_ref[...] + mod_ref[2:3, :] * dx


def _window_attention(x2, q, k, v, kc, vc, sink, mod_l, batch, seq_len, w_bf, b):
    t_rows, d = x2.shape
    lc = kc.shape[0] // batch
    kvw = k.shape[1]
    tq = min(256, seq_len)
    n_qt = seq_len // tq
    wpt = tq // WINDOW
    nwb = t_rows // WINDOW
    qrow = lambda w: pl.BlockSpec((tq, w), lambda bi, i: (bi * n_qt + i, 0))
    prev = pl.BlockSpec((WINDOW, kvw), lambda bi, i: (jnp.maximum((bi * n_qt + i) * wpt - 1, 0), 0))
    nxt = pl.BlockSpec((WINDOW, kvw), lambda bi, i: (jnp.minimum((bi * n_qt + i + 1) * wpt, nwb - 1), 0))
    ctxb = pl.BlockSpec((lc, kvw), lambda bi, i: (bi, 0))
    return pl.pallas_call(
        functools.partial(_attn_kernel, tq=tq, n_qt=n_qt),
        grid=(batch, n_qt),
        in_specs=[pl.BlockSpec(memory_space=pltpu.SMEM),
                  qrow(d), qrow(q.shape[1]),
                  prev, qrow(kvw), nxt, prev, qrow(kvw), nxt, ctxb, ctxb,
                  pl.BlockSpec((None, MOD_CHUNKS, d), lambda bi, i: (bi, 0, 0)),
                  pl.BlockSpec((d, d), lambda bi, i: (0, 0)), pl.BlockSpec((1, d), lambda bi, i: (0, 0))],
        out_specs=qrow(d),
        out_shape=jax.ShapeDtypeStruct((t_rows, d), F32),
        scratch_shapes=[pltpu.VMEM((tq, d), BF16)],
        compiler_params=_params(2),
        name="window_attention",
    )(sink, x2, q, k, k, k, v, v, v, kc, vc, mod_l, w_bf, b.reshape(1, d))


def _latent_mod_index(t, tiles_per_seq):
    return t // tiles_per_seq


def kernel(x, c, ctx, c_ctx, mod_w, mod_b, norm1_w, norm2_w, mlp_w1, mlp_w2, hy_w_in, hy_b_in, hy_conv_w,
           hy_conv_b, hy_f_w1, hy_f_b1, hy_f_freq1, hy_f_w2, hy_f_b2, hy_f_freq2, hy_f_w3, hy_skip, hy_w_out,
           hy_b_out, at_w_qkv, at_b_qkv, at_q_norm, at_k_norm, at_sink, at_w_out, at_b_out):
    batch, seq_len, d = x.shape
    lc = ctx.shape[1]
    ctx_row = batch

    def ctx_mod_index(t, tiles_per_seq):
        return ctx_row

    cond_rows = -(-(batch + 1) // 8) * 8
    cc = jnp.zeros((cond_rows, d), F32).at[:batch].set(c).at[batch].set(c_ctx)
    mod = _modulation(cc, mod_w, mod_b)

    x2 = x.reshape(batch * seq_len, d)
    c2 = ctx.reshape(batch * lc, d)

    w_in = hy_w_in[0].astype(BF16)
    w_out = hy_w_out[0].astype(BF16)
    w1 = mlp_w1[0].astype(BF16)
    w2 = mlp_w2[0].astype(BF16)
    fpar = (hy_f_w1[0], hy_f_b1[0], hy_f_freq1[0], hy_f_w2[0], hy_f_b2[0], hy_f_freq2[0], hy_f_w3[0])
    def hyena_layer(tok, n_seq, idx_fn, conv):
        x0, u = _hyena_inproj(tok, mod[0], idx_fn, n_seq, norm1_w[0], w_in, hy_b_in[0], hy_conv_w[0],
                              hy_conv_b[0])
        kc = _hyena_filter(n_seq, d, *fpar)
        yc = conv(u.reshape(batch, n_seq, d), kc).reshape(batch * n_seq, d)
        tok = _hyena_outproj(tok, x0, u, yc, mod[0], idx_fn, n_seq, hy_skip[0], w_out, hy_b_out[0])
        return _mlp(tok, mod[0], idx_fn, n_seq, norm2_w[0], w1, w2)

    x2 = hyena_layer(x2, seq_len, _latent_mod_index, _long_conv)
    c2 = hyena_layer(c2, lc, ctx_mod_index, _short_seq_conv)

    w_qkv = at_w_qkv[0].astype(BF16)
    q_width = N_HEADS * HEAD_DIM
    rope = _rope_tables(seq_len)
    q, k, v = _qkv_proj(x2, mod[1], _latent_mod_index, seq_len, norm1_w[1], w_qkv, at_b_qkv[0],
                        at_q_norm[0], at_k_norm[0], True, rope)
    kctx, vctx = _qkv_proj(c2, mod[1], ctx_mod_index, lc, norm1_w[1], w_qkv[:, q_width:],
                           at_b_qkv[0][q_width:], at_q_norm[0], at_k_norm[0], False, None)
    x2 = _window_attention(x2, q, k, v, kctx, vctx, at_sink[0].astype(F32) * LOG2_E, mod[1], batch, seq_len,
                           at_w_out[0].astype(BF16), at_b_out[0])
    x2 = _mlp(x2, mod[1], _latent_mod_index, seq_len, norm2_w[1], mlp_w1[1].astype(BF16),
              mlp_w2[1].astype(BF16))
    return x2.reshape(batch, seq_len, d)
```

```python
import functools
import math

import numpy as np
import jax
import jax.numpy as jnp
from jax import lax
from jax.experimental import pallas as pl
from jax.experimental.pallas import tpu as pltpu

F32 = jnp.float32
BF16 = jnp.bfloat16

EPS = 1e-6
MOD_CHUNKS = 6
GRID_W = 64
HY_SHORT = 3
HY_BANDS = 16
HY_DECAY_TARGET = 1e-2
HY_MAX_DECAY = math.log(HY_DECAY_TARGET) / 0.3
HY_MIN_DECAY = math.log(HY_DECAY_TARGET) / 1.5
N_HEADS = 16
N_KV_HEADS = 4
GROUP = N_HEADS // N_KV_HEADS
HEAD_DIM = 64
WINDOW = 128
ROPE_BASE = 10000.0
ROPE_NFREQ = HEAD_DIM // 4
LOG2_E = math.log2(math.e)
Q_SCALE_LOG2 = HEAD_DIM ** -0.5 * LOG2_E

V7X_VMEM_BYTES = 64 * 1024 * 1024
VMEM_LIMIT = V7X_VMEM_BYTES - 8 * 1024 * 1024
LANES = 128
BF16_SUBLANES = 16
FFT_N2 = 128
NEG_BIG = -1e30


def _params(n_axes):
    return pltpu.CompilerParams(dimension_semantics=("arbitrary",) * n_axes,
                                vmem_limit_bytes=VMEM_LIMIT)


def _split_bf16(a):
    hi = a.astype(BF16)
    lo = (a - hi.astype(F32)).astype(BF16)
    return hi, lo


def _dot(a, b):
    return jnp.dot(a, b, preferred_element_type=F32)


def _dot3(a, b):
    ah, al = _split_bf16(a)
    bh, bl = _split_bf16(b)
    return _dot(ah, bh) + (_dot(ah, bl) + _dot(al, bh))


def _norm_mod(x, g, shift, scale):
    ms = jnp.mean(x * x, axis=-1, keepdims=True)
    y = x * lax.rsqrt(ms + EPS) * g
    return y * (1.0 + scale) + shift


def _full(shape):
    nd = len(shape)
    return pl.BlockSpec(shape, lambda *_: (0,) * nd)


def _mod_kernel(c_ref, w_ref, b_ref, o_ref):
    c = c_ref[...]
    s = c * jax.nn.sigmoid(c)
    o_ref[...] = _dot3(s, w_ref[...]) + b_ref[...]


def _modulation(cc, mod_w, mod_b):
    depth, d, n = mod_w.shape
    r = cc.shape[0]
    tn = 1536
    out = pl.pallas_call(
        _mod_kernel,
        grid=(depth, n // tn),
        in_specs=[pl.BlockSpec((r, d), lambda i, j: (0, 0)),
                  pl.BlockSpec((None, d, tn), lambda i, j: (i, 0, j)),
                  pl.BlockSpec((None, 1, tn), lambda i, j: (i, 0, j))],
        out_specs=pl.BlockSpec((None, r, tn), lambda i, j: (i, 0, j)),
        out_shape=jax.ShapeDtypeStruct((depth, r, n), F32),
        compiler_params=_params(2),
        name="modulation",
    )(cc, mod_w, mod_b.reshape(depth, 1, n))
    return out.reshape(depth, r, MOD_CHUNKS, d)


def _inproj_kernel(x_ref, xp_ref, xn_ref, mod_ref, g_ref, w_ref, b_ref, cw_ref, cb_ref,
                   x0_ref, u_ref, p0_ref, p1_ref, p2_ref, *, tm, tiles_per_seq, halo, cw_cols):
    t = pl.program_id(0)
    first = (t % tiles_per_seq) == 0
    last = (t % tiles_per_seq) == tiles_per_seq - 1
    g = g_ref[...]
    shift = mod_ref[0:1, :]
    scale = mod_ref[1:2, :]

    def nm(v):
        return _norm_mod(v, g, shift, scale).astype(BF16)

    h = jnp.concatenate([nm(xp_ref[...]), nm(x_ref[...]), nm(xn_ref[...])], axis=0)
    d = x_ref.shape[1]
    row = lax.broadcasted_iota(jnp.int32, (halo, cw_cols), 0)
    kill_top = jnp.logical_and(first, row == halo - 1)
    kill_bot = jnp.logical_and(last, row == 0)

    def conv(col0, p_ref):
        p = _dot(h, w_ref[:, col0:col0 + cw_cols]) + b_ref[:, col0:col0 + cw_cols]
        p_ref[0:halo, :] = jnp.where(kill_top, 0.0, p[0:halo])
        p_ref[halo:tm + halo, :] = p[halo:tm + halo]
        p_ref[tm + halo:tm + 2 * halo, :] = jnp.where(kill_bot, 0.0, p[tm + halo:tm + 2 * halo])
        out = cb_ref[:, col0:col0 + cw_cols]
        for k in range(HY_SHORT):
            out = out + p_ref[halo - 1 + k:halo - 1 + k + tm, :] * cw_ref[k:k + 1, col0:col0 + cw_cols]
        return out

    for c0 in range(0, d, cw_cols):
        x0_ref[:, c0:c0 + cw_cols] = conv(c0, p0_ref).astype(BF16)
        u_ref[:, c0:c0 + cw_cols] = (conv(d + c0, p1_ref) * conv(2 * d + c0, p2_ref)).astype(BF16)


def _hyena_inproj(x2, mod_l, mod_index, seq_len, g, w_bf, b, conv_w, conv_b):
    t_rows, d = x2.shape
    tm = min(512, seq_len)
    halo = 16
    tps = seq_len // tm
    nblk = t_rows // halo
    cw_cols = 256
    kern = functools.partial(_inproj_kernel, tm=tm, tiles_per_seq=tps, halo=halo, cw_cols=cw_cols)
    return pl.pallas_call(
        kern,
        grid=(t_rows // tm,),
        in_specs=[pl.BlockSpec((tm, d), lambda t: (t, 0)),
                  pl.BlockSpec((halo, d), lambda t: (jnp.maximum(t * (tm // halo) - 1, 0), 0)),
                  pl.BlockSpec((halo, d), lambda t: (jnp.minimum((t + 1) * (tm // halo), nblk - 1), 0)),
                  pl.BlockSpec((None, MOD_CHUNKS, d), lambda t: (mod_index(t, tps), 0, 0)),
                  _full((1, d)), _full((d, 3 * d)), _full((1, 3 * d)),
                  _full((HY_SHORT, 3 * d)), _full((1, 3 * d))],
        out_specs=[pl.BlockSpec((tm, d), lambda t: (t, 0)),
                   pl.BlockSpec((tm, d), lambda t: (t, 0))],
        out_shape=[jax.ShapeDtypeStruct((t_rows, d), BF16),
                   jax.ShapeDtypeStruct((t_rows, d), BF16)],
        scratch_shapes=[pltpu.VMEM((tm + 2 * halo, cw_cols), F32)] * 3,
        compiler_params=_params(1),
        name="hyena_inproj",
    )(x2, x2, x2, mod_l, g.reshape(1, d), w_bf, b.reshape(1, 3 * d), conv_w, conv_b.reshape(1, 3 * d))


def _filter_hidden_kernel(z_ref, w1_ref, b1_ref, f1_ref, w2_ref, b2_ref, f2_ref, o_ref):
    h = jnp.sin(f1_ref[...] * (_dot3(z_ref[...], w1_ref[...]) + b1_ref[...]))
    o_ref[...] = jnp.sin(f2_ref[...] * (_dot3(h, w2_ref[...]) + b2_ref[...]))


def _filter_out_kernel(h_ref, w3_ref, dl_ref, o_ref, sum_ref, *, seq_len):
    i = pl.program_id(1)
    tr, td = o_ref.shape
    r = lax.broadcasted_iota(jnp.int32, (tr, td), 0) + i * tr
    pos = jnp.where(r < seq_len, r, 2 * seq_len - r)
    t = pos.astype(F32) / seq_len
    kc = _dot3(h_ref[...], w3_ref[...]) * jnp.exp(-t * dl_ref[...])
    kc = jnp.where(r == seq_len, 0.0, kc)
    o_ref[...] = kc

    @pl.when(i == 0)
    def _():
        sum_ref[...] = jnp.zeros_like(sum_ref)

    sum_ref[...] += jnp.sum(jnp.abs(kc), axis=0, keepdims=True)


def _filter_features(seq_len):
    f32 = jnp.float32
    pos_f = jnp.arange(seq_len, dtype=f32)
    pos_b = ((seq_len - jnp.arange(seq_len)) % seq_len).astype(f32)
    pos = jnp.concatenate([pos_f, pos_b])
    t = pos / seq_len
    bands = jnp.linspace(1e-4, HY_BANDS - 1, HY_BANDS, dtype=f32)
    ang = (2.0 * math.pi / seq_len) * pos[:, None] * bands[None, :]
    return jnp.concatenate([t[:, None], jnp.cos(ang), -jnp.sin(ang)], axis=-1)


def _hyena_filter(seq_len, d, f_w1, f_b1, f_fr1, f_w2, f_b2, f_fr2, f_w3):
    z = _filter_features(seq_len)
    emb = z.shape[1]
    hid = f_w1.shape[1]
    rows = 2 * seq_len
    tr = min(2048, seq_len)
    h2 = pl.pallas_call(
        _filter_hidden_kernel,
        grid=(rows // tr,),
        in_specs=[pl.BlockSpec((tr, emb), lambda i: (i, 0)),
                  _full((emb, hid)), _full((1, hid)), _full((1, hid)),
                  _full((hid, hid)), _full((1, hid)), _full((1, hid))],
        out_specs=pl.BlockSpec((tr, hid), lambda i: (i, 0)),
        out_shape=jax.ShapeDtypeStruct((rows, hid), F32),
        compiler_params=_params(1),
        name="hyena_filter_hidden",
    )(z, f_w1, f_b1.reshape(1, hid), f_fr1.reshape(1, hid), f_w2, f_b2.reshape(1, hid),
      f_fr2.reshape(1, hid))
    deltas = jnp.abs(jnp.linspace(HY_MIN_DECAY, HY_MAX_DECAY, d, dtype=F32)).reshape(1, d)
    td = 256
    nd = d // td
    tiles_per_half = seq_len // tr
    return pl.pallas_call(
        functools.partial(_filter_out_kernel, seq_len=seq_len),
        grid=(nd, rows // tr),
        in_specs=[pl.BlockSpec((tr, hid), lambda j, i: (i, 0)),
                  pl.BlockSpec((hid, td), lambda j, i: (0, (i // tiles_per_half) * nd + j)),
                  pl.BlockSpec((1, td), lambda j, i: (0, j))],
        out_specs=[pl.BlockSpec((tr, td), lambda j, i: (i, j)),
                   pl.BlockSpec((1, td), lambda j, i: (0, j))],
        out_shape=[jax.ShapeDtypeStruct((rows, d), F32), jax.ShapeDtypeStruct((1, d), F32)],
        compiler_params=_params(2),
        name="hyena_filter_out",
    )(h2, f_w3, deltas)


def _dft_tables(n1):
    n2 = FFT_N2
    n = n1 * n2
    h1 = n1 // 2
    k1 = np.concatenate([np.arange(0, h1 + 1), np.arange(1, h1)])
    is_im = np.concatenate([np.zeros(h1 + 1, bool), np.ones(h1 - 1, bool)])
    th = 2.0 * np.pi * np.outer(k1, np.arange(n1)) / n1
    f1 = np.where(is_im[:, None], -np.sin(th), np.cos(th))
    wgt = np.where((k1 == 0) | (k1 == h1), 1.0, 2.0)[:, None] / n
    f1inv = (np.where(is_im[:, None], -np.sin(th), np.cos(th)) * wgt).T
    m = np.arange(n2)
    k2 = np.arange(n2)

    def g(k1v):
        ph = 2.0 * np.pi * np.outer(k2 * n1 + k1v, m) / n
        return np.cos(ph), -np.sin(ph)

    mf = np.zeros((h1, 2 * n2, 2 * n2))
    for j in range(1, h1):
        gr, gi = g(j)
        mf[j] = np.block([[gr, -gi], [gi, gr]])
    g0r, g0i = g(0)
    ghr, ghi = g(h1)
    zz = np.zeros((n2, n2))
    m0 = np.block([[g0r, zz], [g0i, zz], [zz, ghr], [zz, ghi]])
    mfi = np.transpose(mf, (0, 2, 1))
    m0i = m0.T
    as32 = lambda a: jnp.asarray(a.astype(np.float32))
    return dict(f1=as32(f1), f1inv=as32(f1inv), mf=as32(mf), mfi=as32(mfi), m0=as32(m0), m0i=as32(m0i))


def _stage1_kernel(f_ref, x_ref, o_ref):
    xt = jnp.swapaxes(x_ref[...].astype(BF16), 0, 1)
    f = f_ref[...]
    out = jnp.stack([_dot(f, xt[t]).astype(o_ref.dtype) for t in range(xt.shape[0])], axis=0)
    o_ref[...] = jnp.swapaxes(out, 0, 1)


def _dft_stage1(f_bf, x4, out_dtype):
    bx, r, n2, d = x4.shape
    m = f_bf.shape[0]
    tt = BF16_SUBLANES
    return pl.pallas_call(
        _stage1_kernel,
        grid=(bx, n2 // tt),
        in_specs=[_full((m, r)), pl.BlockSpec((None, r, tt, d), lambda b, j: (b, 0, j, 0))],
        out_specs=pl.BlockSpec((None, m, tt, d), lambda b, j: (b, 0, j, 0)),
        out_shape=jax.ShapeDtypeStruct((bx, m, n2, d), out_dtype),
        compiler_params=_params(2),
        name="dft_stage1",
    )(f_bf, x4)


def _filter_spec_kernel(a_ref, nrm_ref, mf_ref, m0_ref, ks_ref, kh_ref):
    j = pl.program_id(0)
    n2 = FFT_N2
    a = a_ref[...].reshape(2 * n2, a_ref.shape[-1])
    nrm = nrm_ref[...]

    @pl.when(j == 0)
    def _():
        x = _dot(m0_ref[...], a) / nrm
        ks_ref[...] = x[0:2 * n2].reshape(ks_ref.shape)
        kh_ref[...] = x[2 * n2:4 * n2].reshape(kh_ref.shape)

    @pl.when(j > 0)
    def _():
        ks_ref[...] = (_dot(mf_ref[...], a) / nrm).reshape(ks_ref.shape)


def _filter_spectrum(ak, nrm, tabs):
    _, h1, n2, d = ak.shape
    return pl.pallas_call(
        _filter_spec_kernel,
        grid=(h1,),
        in_specs=[pl.BlockSpec((2, None, n2, d), lambda j: (0, j, 0, 0)),
                  _full((1, d)),
                  pl.BlockSpec((None, 2 * n2, 2 * n2), lambda j: (j, 0, 0)),
                  _full((4 * n2, 2 * n2))],
        out_specs=[pl.BlockSpec((None, 2, n2, d), lambda j: (j, 0, 0, 0)),
                   _full((2, n2, d))],
        out_shape=[jax.ShapeDtypeStruct((h1, 2, n2, d), F32),
                   jax.ShapeDtypeStruct((2, n2, d), F32)],
        compiler_params=_params(1),
        name="hyena_filter_spectrum",
    )(ak, nrm, tabs["mf"].astype(BF16), tabs["m0"].astype(BF16))


def _cmul(xr, xi, kr, ki):
    return xr * kr - xi * ki, xr * ki + xi * kr


def _stage2_kernel(a_ref, mf_ref, mfi_ref, m0_ref, m0i_ref, ks_ref, kh_ref, o_ref):
    j = pl.program_id(0)
    n2 = FFT_N2
    nb = a_ref.shape[0]
    d = a_ref.shape[-1]

    @pl.when(j == 0)
    def _():
        for b in range(nb):
            a = a_ref[b].reshape(2 * n2, d)
            x = _dot(m0_ref[...], a)
            y0r, y0i = _cmul(x[0:n2], x[n2:2 * n2], ks_ref[0], ks_ref[1])
            y1r, y1i = _cmul(x[2 * n2:3 * n2], x[3 * n2:4 * n2], kh_ref[0], kh_ref[1])
            y = jnp.concatenate([y0r, y0i, y1r, y1i], axis=0).astype(BF16)
            o_ref[b] = _dot(m0i_ref[...], y).reshape(2, n2, d).astype(o_ref.dtype)

    @pl.when(j > 0)
    def _():
        for b in range(nb):
            a = a_ref[b].reshape(2 * n2, d)
            x = _dot(mf_ref[...], a)
            yr, yi = _cmul(x[0:n2], x[n2:2 * n2], ks_ref[0], ks_ref[1])
            y = jnp.concatenate([yr, yi], axis=0).astype(BF16)
            o_ref[b] = _dot(mfi_ref[...], y).reshape(2, n2, d).astype(o_ref.dtype)


def _dft_stage2(a5, tabs, ks, kh):
    nb, _, h1, n2, d = a5.shape
    blk = pl.BlockSpec((nb, 2, None, n2, d), lambda j: (0, 0, j, 0, 0))
    return pl.pallas_call(
        _stage2_kernel,
        grid=(h1,),
        in_specs=[blk,
                  pl.BlockSpec((None, 2 * n2, 2 * n2), lambda j: (j, 0, 0)),
                  pl.BlockSpec((None, 2 * n2, 2 * n2), lambda j: (j, 0, 0)),
                  _full((4 * n2, 2 * n2)), _full((2 * n2, 4 * n2)),
                  pl.BlockSpec((None, 2, n2, d), lambda j: (j, 0, 0, 0)),
                  _full((2, n2, d))],
        out_specs=blk,
        out_shape=jax.ShapeDtypeStruct(a5.shape, BF16),
        compiler_params=_params(1),
        name="dft_stage2",
    )(a5, tabs["mf"].astype(BF16), tabs["mfi"].astype(BF16), tabs["m0"].astype(BF16),
      tabs["m0i"].astype(BF16), ks, kh)


def _long_conv(u, kc, nrm):
    b, seq_len, d = u.shape
    n2 = FFT_N2
    n1 = 2 * seq_len // n2
    h1 = n1 // 2
    tabs = _dft_tables(n1)
    f1 = tabs["f1"].astype(BF16)
    ak = _dft_stage1(f1, kc.reshape(1, n1, n2, d), BF16)
    ks, kh = _filter_spectrum(ak.reshape(2, h1, n2, d), nrm, tabs)
    a = _dft_stage1(f1[:, :h1], u.reshape(b, h1, n2, d), BF16)
    bv = _dft_stage2(a.reshape(b, 2, h1, n2, d), tabs, ks, kh)
    y = _dft_stage1(tabs["f1inv"][:h1].astype(BF16), bv.reshape(b, n1, n2, d), BF16)
    return y.reshape(b, seq_len, d)


def _ctxconv_kernel(u_ref, kc_ref, nrm_ref, cf_ref, ci_ref, o_ref, *, seq_len):
    nf = 2 * seq_len
    kh, kl = _split_bf16(kc_ref[...])
    ksp = (_dot(cf_ref[...], kh) + _dot(cf_ref[...], kl)) / nrm_ref[...]
    kr, ki = ksp[0:nf], ksp[nf:2 * nf]
    for b in range(u_ref.shape[0]):
        x = _dot(cf_ref[:, 0:seq_len], u_ref[b])
        yr, yi = _cmul(x[0:nf], x[nf:2 * nf], kr, ki)
        y = jnp.concatenate([yr, yi], axis=0).astype(BF16)
        o_ref[b] = _dot(ci_ref[...], y).astype(o_ref.dtype)


def _short_seq_conv(u, kc, nrm):
    b, seq_len, d = u.shape
    nf = 2 * seq_len
    ph = 2.0 * np.pi * np.outer(np.arange(nf), np.arange(nf)) / nf
    cf = jnp.asarray(np.concatenate([np.cos(ph), -np.sin(ph)], axis=0).astype(np.float32))
    ci = jnp.asarray((np.concatenate([np.cos(ph), -np.sin(ph)], axis=1)[:seq_len] / nf).astype(np.float32))
    return pl.pallas_call(
        functools.partial(_ctxconv_kernel, seq_len=seq_len),
        grid=(1,),
        in_specs=[_full((b, seq_len, d)), _full((nf, d)), _full((1, d)), _full((2 * nf, nf)),
                  _full((seq_len, 2 * nf))],
        out_specs=_full((b, seq_len, d)),
        out_shape=jax.ShapeDtypeStruct((b, seq_len, d), BF16),
        compiler_params=_params(1),
        name="hyena_ctx_conv",
    )(u, kc, nrm, cf.astype(BF16), ci.astype(BF16))


def _mlp_body(x, mod_ref, g_ref, w1_ref, w2_ref, ff_chunk):
    h = _norm_mod(x, g_ref[...], mod_ref[3:4, :], mod_ref[4:5, :]).astype(BF16)
    acc = jnp.zeros(x.shape, F32)
    for c0 in range(0, w1_ref.shape[1], ff_chunk):
        a = jnp.maximum(_dot(h, w1_ref[:, c0:c0 + ff_chunk]), 0.0)
        acc = acc + _dot((a * a).astype(BF16), w2_ref[c0:c0 + ff_chunk, :])
    return x + mod_ref[5:6, :] * acc


def _hy_out_mlp_kernel(x_ref, x0_ref, u_ref, yc_ref, mod_ref, skip_ref, w_ref, b_ref, g_ref, w1_ref, w2_ref,
                       o_ref, *, ff_chunk):
    u = u_ref[...].astype(F32)
    y = x0_ref[...].astype(F32) * (yc_ref[...].astype(F32) + u * skip_ref[...])
    dx = _dot(y.astype(BF16), w_ref[...]) + b_ref[...]
    x = x_ref[...] + mod_ref[2:3, :] * dx
    o_ref[...] = _mlp_body(x, mod_ref, g_ref, w1_ref, w2_ref, ff_chunk)


def _resident(shape):
    nd = len(shape)
    return pl.BlockSpec(shape, lambda *_: (0,) * nd, pipeline_mode=pl.Buffered(1))


MLP_FF_CHUNK = 1024


def _hyena_out_mlp(x2, x0, u, yc, mod_l, mod_index, seq_len, skip, w_bf, b, g, w1_bf, w2_bf):
    t_rows, d = x2.shape
    ff = w1_bf.shape[1]
    tm = min(512, seq_len)
    tps = seq_len // tm
    row = pl.BlockSpec((tm, d), lambda t: (t, 0))
    return pl.pallas_call(
        functools.partial(_hy_out_mlp_kernel, ff_chunk=MLP_FF_CHUNK),
        grid=(t_rows // tm,),
        in_specs=[row, row, row, row,
                  pl.BlockSpec((None, MOD_CHUNKS, d), lambda t: (mod_index(t, tps), 0, 0)),
                  _full((1, d)), _resident((d, d)), _full((1, d)),
                  _full((1, d)), _resident((d, ff)), _resident((ff, d))],
        out_specs=row,
        out_shape=jax.ShapeDtypeStruct((t_rows, d), F32),
        compiler_params=_params(1),
        name="hyena_out_mlp",
    )(x2, x0, u, yc, mod_l, skip.reshape(1, d), w_bf, b.reshape(1, d), g.reshape(1, d), w1_bf, w2_bf)


def _mlp_kernel(x_ref, mod_ref, g_ref, w1_ref, w2_ref, o_ref, *, ff_chunk):
    o_ref[...] = _mlp_body(x_ref[...], mod_ref, g_ref, w1_ref, w2_ref, ff_chunk)


def _mlp(x2, mod_l, mod_index, seq_len, g, w1_bf, w2_bf):
    t_rows, d = x2.shape
    ff = w1_bf.shape[1]
    tm = min(512, seq_len)
    tps = seq_len // tm
    row = pl.BlockSpec((tm, d), lambda t: (t, 0))
    return pl.pallas_call(
        functools.partial(_mlp_kernel, ff_chunk=MLP_FF_CHUNK),
        grid=(t_rows // tm,),
        in_specs=[row,
                  pl.BlockSpec((None, MOD_CHUNKS, d), lambda t: (mod_index(t, tps), 0, 0)),
                  _full((1, d)), _resident((d, ff)), _resident((ff, d))],
        out_specs=row,
        out_shape=jax.ShapeDtypeStruct((t_rows, d), F32),
        compiler_params=_params(1),
        name="mlp",
    )(x2, mod_l, g.reshape(1, d), w1_bf, w2_bf)


def _rope(v, cos, sin_signed, hi_half):
    partner = jnp.where(hi_half, pltpu.roll(v, 16, axis=1), pltpu.roll(v, LANES - 16, axis=1))
    return v * cos + partner * sin_signed


def _qkv_kernel(*refs, with_q, with_rope, q_width, kv_width):
    x_ref, mod_ref, g_ref, w_ref, b_ref, bd_ref, qn_ref, kn_ref = refs[:8]
    idx = 8
    if with_rope:
        cos_ref, sin_ref = refs[idx:idx + 2]
        idx += 2
    outs = refs[idx:]
    if with_q:
        q_ref, kv_ref = outs
    else:
        (kv_ref,) = outs
    h = _norm_mod(x_ref[...], g_ref[...], mod_ref[0:1, :], mod_ref[1:2, :]).astype(BF16)
    tm = h.shape[0]
    if with_rope:
        cos = cos_ref[...]
        sin = sin_ref[...]
        hi_half = (lax.broadcasted_iota(jnp.int32, (tm, LANES), 1) & 16) != 0

    def head_norm(p, w, bd):
        ms = _dot((p * p).astype(BF16), bd)
        return p * lax.rsqrt(ms + EPS) * w

    def finish(p, out_ref, scale):
        if scale != 1.0:
            p = p * scale
        if with_rope:
            for c0 in range(0, p.shape[1], LANES):
                out_ref[:, c0:c0 + LANES] = _rope(p[:, c0:c0 + LANES], cos, sin, hi_half).astype(out_ref.dtype)
        else:
            out_ref[:, 0:p.shape[1]] = p.astype(out_ref.dtype)

    off = 0
    if with_q:
        q = _dot(h, w_ref[:, 0:q_width]) + b_ref[:, 0:q_width]
        finish(head_norm(q, qn_ref[...], bd_ref[...]), q_ref, Q_SCALE_LOG2)
        off = q_width
    k = _dot(h, w_ref[:, off:off + kv_width]) + b_ref[:, off:off + kv_width]
    finish(head_norm(k, kn_ref[...], bd_ref[0:kv_width, 0:kv_width]), kv_ref, 1.0)
    v = _dot(h, w_ref[:, off + kv_width:off + 2 * kv_width]) + b_ref[:, off + kv_width:off + 2 * kv_width]
    kv_ref[:, kv_width:2 * kv_width] = v.astype(kv_ref.dtype)


def _rope_tables(seq_len):
    rows = seq_len // GRID_W
    row = jnp.repeat(jnp.arange(rows, dtype=F32), GRID_W)
    col = jnp.tile(jnp.arange(GRID_W, dtype=F32), rows)
    inv = ROPE_BASE ** (-jnp.arange(ROPE_NFREQ, dtype=F32) / ROPE_NFREQ)
    ar = row[:, None] * inv[None, :]
    ac = col[:, None] * inv[None, :]
    cos = jnp.concatenate([jnp.cos(ar), jnp.cos(ar), jnp.cos(ac), jnp.cos(ac)], axis=-1)
    sin = jnp.concatenate([-jnp.sin(ar), jnp.sin(ar), -jnp.sin(ac), jnp.sin(ac)], axis=-1)
    return jnp.tile(cos, (1, LANES // HEAD_DIM)), jnp.tile(sin, (1, LANES // HEAD_DIM))


def _qkv_proj(x2, mod_l, mod_index, seq_len, g, w_bf, b, qn, kn, with_q, rope):
    t_rows, d = x2.shape
    q_width = N_HEADS * HEAD_DIM
    kv_width = N_KV_HEADS * HEAD_DIM
    width = w_bf.shape[1]
    tm = min(512, seq_len)
    tps = seq_len // tm
    bd = jnp.asarray(np.kron(np.eye(N_HEADS), np.full((HEAD_DIM, HEAD_DIM), 1.0 / HEAD_DIM)).astype(np.float32))
    row = lambda w: pl.BlockSpec((tm, w), lambda t: (t, 0))
    in_specs = [row(d),
                pl.BlockSpec((None, MOD_CHUNKS, d), lambda t: (mod_index(t, tps), 0, 0)),
                _full((1, d)), _full((d, width)), _full((1, width)), _full((q_width, q_width)),
                _full((1, q_width)), _full((1, kv_width))]
    args = [x2, mod_l, g.reshape(1, d), w_bf, b.reshape(1, width), bd.astype(BF16),
            jnp.tile(qn, N_HEADS).reshape(1, q_width), jnp.tile(kn, N_KV_HEADS).reshape(1, kv_width)]
    if rope is not None:
        in_specs += [pl.BlockSpec((tm, LANES), lambda t: (t % tps, 0))] * 2
        args += list(rope)
    out_specs = [row(2 * kv_width)]
    out_shape = [jax.ShapeDtypeStruct((t_rows, 2 * kv_width), BF16)]
    if with_q:
        out_specs = [row(q_width)] + out_specs
        out_shape = [jax.ShapeDtypeStruct((t_rows, q_width), BF16)] + out_shape
    return pl.pallas_call(
        functools.partial(_qkv_kernel, with_q=with_q, with_rope=rope is not None,
                          q_width=q_width, kv_width=kv_width),
        grid=(t_rows // tm,),
        in_specs=in_specs, out_specs=out_specs, out_shape=out_shape,
        compiler_params=_params(1),
        name="qkv_proj",
    )(*args)


def _attn_kernel(sink_ref, x_ref, q_ref, kvp_ref, kvm_ref, kvn_ref, kvc_ref,
                 mod_ref, w_ref, b_ref, o_ref, att_ref, *, tq, n_qt):
    i = pl.program_id(1)
    lc = kvc_ref.shape[0]
    kvw = N_KV_HEADS * HEAD_DIM
    nw = 3 * WINDOW
    nk = nw + lc
    n_rb = tq // WINDOW
    r = lax.broadcasted_iota(jnp.int32, (WINDOW, nk), 0)
    c = lax.broadcasted_iota(jnp.int32, (WINDOW, nk), 1)
    rel = c - WINDOW - r
    band = jnp.logical_and(rel >= -WINDOW, rel <= WINDOW)
    biases = []
    for rb in range(n_rb):
        ok = band
        if rb == 0:
            ok = jnp.logical_and(ok, jnp.logical_or(i > 0, c >= WINDOW))
        if rb == n_rb - 1:
            ok = jnp.logical_and(ok, jnp.logical_or(i < n_qt - 1, c < 2 * WINDOW))
        ok = jnp.logical_or(ok, c >= nw)
        biases.append(jnp.where(ok, 0.0, NEG_BIG))
    for hk in range(N_KV_HEADS):
        ls = slice(hk * HEAD_DIM, (hk + 1) * HEAD_DIM)
        lv = slice(kvw + hk * HEAD_DIM, kvw + (hk + 1) * HEAD_DIM)
        kw = jnp.concatenate([kvp_ref[:, ls], kvm_ref[:, ls], kvn_ref[:, ls]], axis=0)
        vw = jnp.concatenate([kvp_ref[:, lv], kvm_ref[:, lv], kvn_ref[:, lv]], axis=0)
        kcx = kvc_ref[:, ls]
        vcx = kvc_ref[:, lv]
        heads = [hk * GROUP + gq for gq in range(GROUP)]
        for rb in range(n_rb):
            rows = slice(rb * WINDOW, (rb + 1) * WINDOW)
            kk = jnp.concatenate([kw[rb * WINDOW:rb * WINDOW + nw], kcx], axis=0)
            vv = jnp.concatenate([vw[rb * WINDOW:rb * WINDOW + nw], vcx], axis=0)
            ss = [lax.dot_general(q_ref[rows, hd * HEAD_DIM:(hd + 1) * HEAD_DIM], kk, (((1,), (1,)), ((), ())),
                                  preferred_element_type=F32) + biases[rb] for hd in heads]
            ms = [jnp.maximum(jnp.max(s, axis=-1, keepdims=True), sink_ref[hd]) for s, hd in zip(ss, heads)]
            ps = [jnp.exp2(s - m) for s, m in zip(ss, ms)]
            dens = [jnp.sum(p, axis=-1, keepdims=True) + jnp.exp2(sink_ref[hd] - m)
                    for p, m, hd in zip(ps, ms, heads)]
            for p, den, hd in zip(ps, dens, heads):
                o = _dot(p.astype(BF16), vv) / den
                att_ref[rows, hd * HEAD_DIM:(hd + 1) * HEAD_DIM] = o.astype(BF16)
    dx = _dot(att_ref[...], w_ref[...]) + b_ref[...]
    o_ref[...] = x_ref[...] + mod_ref[2:3, :] * dx


def _window_attention(x2, q, kv, kvc, sink, mod_l, batch, seq_len, w_bf, b):
    t_rows, d = x2.shape
    lc = kvc.shape[0] // batch
    kvw = kv.shape[1]
    tq = min(512, seq_len)
    n_qt = seq_len // tq
    wpt = tq // WINDOW
    nwb = t_rows // WINDOW
    qrow = lambda w: pl.BlockSpec((tq, w), lambda bi, i: (bi * n_qt + i, 0))
    prev = pl.BlockSpec((WINDOW, kvw), lambda bi, i: (jnp.maximum((bi * n_qt + i) * wpt - 1, 0), 0))
    nxt = pl.BlockSpec((WINDOW, kvw), lambda bi, i: (jnp.minimum((bi * n_qt + i + 1) * wpt, nwb - 1), 0))
    ctxb = pl.BlockSpec((lc, kvw), lambda bi, i: (bi, 0))
    return pl.pallas_call(
        functools.partial(_attn_kernel, tq=tq, n_qt=n_qt),
        grid=(batch, n_qt),
        in_specs=[pl.BlockSpec(memory_space=pltpu.SMEM),
                  qrow(d), qrow(q.shape[1]),
                  prev, qrow(kvw), nxt, ctxb,
                  pl.BlockSpec((None, MOD_CHUNKS, d), lambda bi, i: (bi, 0, 0)),
                  pl.BlockSpec((d, d), lambda bi, i: (0, 0)), pl.BlockSpec((1, d), lambda bi, i: (0, 0))],
        out_specs=qrow(d),
        out_shape=jax.ShapeDtypeStruct((t_rows, d), F32),
        scratch_shapes=[pltpu.VMEM((tq, d), BF16)],
        compiler_params=_params(2),
        name="window_attention",
    )(sink, x2, q, kv, kv, kv, kvc, mod_l, w_bf, b.reshape(1, d))


def _latent_mod_index(t, tiles_per_seq):
    return t // tiles_per_seq


def kernel(x, c, ctx, c_ctx, mod_w, mod_b, norm1_w, norm2_w, mlp_w1, mlp_w2, hy_w_in, hy_b_in, hy_conv_w,
           hy_conv_b, hy_f_w1, hy_f_b1, hy_f_freq1, hy_f_w2, hy_f_b2, hy_f_freq2, hy_f_w3, hy_skip, hy_w_out,
           hy_b_out, at_w_qkv, at_b_qkv, at_q_norm, at_k_norm, at_sink, at_w_out, at_b_out):
    batch, seq_len, d = x.shape
    lc = ctx.shape[1]
    ctx_row = batch

    def ctx_mod_index(t, tiles_per_seq):
        return ctx_row

    cond_rows = -(-(batch + 1) // 8) * 8
    cc = jnp.zeros((cond_rows, d), F32).at[:batch].set(c).at[batch].set(c_ctx)
    mod = _modulation(cc, mod_w, mod_b)

    x2 = x.reshape(batch * seq_len, d)
    c2 = ctx.reshape(batch * lc, d)

    w_in = hy_w_in[0].astype(BF16)
    w_out = hy_w_out[0].astype(BF16)
    w1 = mlp_w1[0].astype(BF16)
    w2 = mlp_w2[0].astype(BF16)
    fpar = (hy_f_w1[0], hy_f_b1[0], hy_f_freq1[0], hy_f_w2[0], hy_f_b2[0], hy_f_freq2[0], hy_f_w3[0])
    def hyena_layer(tok, n_seq, idx_fn, conv):
        x0, u = _hyena_inproj(tok, mod[0], idx_fn, n_seq, norm1_w[0], w_in, hy_b_in[0], hy_conv_w[0],
                              hy_conv_b[0])
        kc, nrm = _hyena_filter(n_seq, d, *fpar)
        yc = conv(u.reshape(batch, n_seq, d), kc, nrm).reshape(batch * n_seq, d)
        return _hyena_out_mlp(tok, x0, u, yc, mod[0], idx_fn, n_seq, hy_skip[0], w_out, hy_b_out[0],
                              norm2_w[0], w1, w2)

    x2 = hyena_layer(x2, seq_len, _latent_mod_index, _long_conv)
    c2 = hyena_layer(c2, lc, ctx_mod_index, _short_seq_conv)

    w_qkv = at_w_qkv[0].astype(BF16)
    q_width = N_HEADS * HEAD_DIM
    rope = _rope_tables(seq_len)
    q, kv = _qkv_proj(x2, mod[1], _latent_mod_index, seq_len, norm1_w[1], w_qkv, at_b_qkv[0],
                      at_q_norm[0], at_k_norm[0], True, rope)
    (kv_ctx,) = _qkv_proj(c2, mod[1], ctx_mod_index, lc, norm1_w[1], w_qkv[:, q_width:],
                          at_b_qkv[0][q_width:], at_q_norm[0], at_k_norm[0], False, None)
    x2 = _window_attention(x2, q, kv, kv_ctx, at_sink[0].astype(F32) * LOG2_E, mod[1], batch, seq_len,
                           at_w_out[0].astype(BF16), at_b_out[0])
    x2 = _mlp(x2, mod[1], _latent_mod_index, seq_len, norm2_w[1], mlp_w1[1].astype(BF16),
              mlp_w2[1].astype(BF16))
    return x2.reshape(batch, seq_len, d)
```

```python
import functools
import math

import numpy as np
import jax
import jax.numpy as jnp
from jax import lax
from jax.experimental import pallas as pl
from jax.experimental.pallas import tpu as pltpu

F32 = jnp.float32
BF16 = jnp.bfloat16

EPS = 1e-6
MOD_CHUNKS = 6
GRID_W = 64
HY_SHORT = 3
HY_BANDS = 16
HY_DECAY_TARGET = 1e-2
HY_MAX_DECAY = math.log(HY_DECAY_TARGET) / 0.3
HY_MIN_DECAY = math.log(HY_DECAY_TARGET) / 1.5
N_HEADS = 16
N_KV_HEADS = 4
GROUP = N_HEADS // N_KV_HEADS
HEAD_DIM = 64
WINDOW = 128
ROPE_BASE = 10000.0
ROPE_NFREQ = HEAD_DIM // 4
LOG2_E = math.log2(math.e)
Q_SCALE_LOG2 = HEAD_DIM ** -0.5 * LOG2_E

V7X_VMEM_BYTES = 64 * 1024 * 1024
VMEM_LIMIT = V7X_VMEM_BYTES - 8 * 1024 * 1024
LANES = 128
BF16_SUBLANES = 16
FFT_N2 = 128
NEG_BIG = -1e30


def _params(n_axes):
    return pltpu.CompilerParams(dimension_semantics=("arbitrary",) * n_axes,
                                vmem_limit_bytes=VMEM_LIMIT)


def _split_bf16(a):
    hi = a.astype(BF16)
    lo = (a - hi.astype(F32)).astype(BF16)
    return hi, lo


def _dot(a, b):
    return jnp.dot(a, b, preferred_element_type=F32)


def _dot3(a, b):
    ah, al = _split_bf16(a)
    bh, bl = _split_bf16(b)
    return _dot(ah, bh) + (_dot(ah, bl) + _dot(al, bh))


def _norm_mod(x, g, shift, scale):
    ms = jnp.mean(x * x, axis=-1, keepdims=True)
    y = x * lax.rsqrt(ms + EPS) * g
    return y * (1.0 + scale) + shift


def _full(shape):
    nd = len(shape)
    return pl.BlockSpec(shape, lambda *_: (0,) * nd)


def _mod_kernel(c_ref, w_ref, b_ref, o_ref):
    c = c_ref[...]
    s = c * jax.nn.sigmoid(c)
    o_ref[...] = _dot3(s, w_ref[...]) + b_ref[...]


def _modulation(cc, mod_w, mod_b):
    depth, d, n = mod_w.shape
    r = cc.shape[0]
    tn = 1536
    out = pl.pallas_call(
        _mod_kernel,
        grid=(depth, n // tn),
        in_specs=[pl.BlockSpec((r, d), lambda i, j: (0, 0)),
                  pl.BlockSpec((None, d, tn), lambda i, j: (i, 0, j)),
                  pl.BlockSpec((None, 1, tn), lambda i, j: (i, 0, j))],
        out_specs=pl.BlockSpec((None, r, tn), lambda i, j: (i, 0, j)),
        out_shape=jax.ShapeDtypeStruct((depth, r, n), F32),
        compiler_params=_params(2),
        name="modulation",
    )(cc, mod_w, mod_b.reshape(depth, 1, n))
    return out.reshape(depth, r, MOD_CHUNKS, d)


def _inproj_kernel(x_ref, xp_ref, xn_ref, mod_ref, g_ref, w_ref, b_ref, cw_ref, cb_ref,
                   x0_ref, u_ref, p0_ref, p1_ref, p2_ref, *, tm, tiles_per_seq, halo, cw_cols):
    t = pl.program_id(0)
    first = (t % tiles_per_seq) == 0
    last = (t % tiles_per_seq) == tiles_per_seq - 1
    g = g_ref[...]
    shift = mod_ref[0:1, :]
    scale = mod_ref[1:2, :]

    def nm(v):
        return _norm_mod(v, g, shift, scale).astype(BF16)

    h = jnp.concatenate([nm(xp_ref[...]), nm(x_ref[...]), nm(xn_ref[...])], axis=0)
    d = x_ref.shape[1]
    row = lax.broadcasted_iota(jnp.int32, (halo, cw_cols), 0)
    kill_top = jnp.logical_and(first, row == halo - 1)
    kill_bot = jnp.logical_and(last, row == 0)

    def conv(col0, p_ref):
        p = _dot(h, w_ref[:, col0:col0 + cw_cols]) + b_ref[:, col0:col0 + cw_cols]
        p_ref[0:halo, :] = jnp.where(kill_top, 0.0, p[0:halo])
        p_ref[halo:tm + halo, :] = p[halo:tm + halo]
        p_ref[tm + halo:tm + 2 * halo, :] = jnp.where(kill_bot, 0.0, p[tm + halo:tm + 2 * halo])
        out = cb_ref[:, col0:col0 + cw_cols]
        for k in range(HY_SHORT):
            out = out + p_ref[halo - 1 + k:halo - 1 + k + tm, :] * cw_ref[k:k + 1, col0:col0 + cw_cols]
        return out

    for c0 in range(0, d, cw_cols):
        x0_ref[:, c0:c0 + cw_cols] = conv(c0, p0_ref).astype(BF16)
        u_ref[:, c0:c0 + cw_cols] = (conv(d + c0, p1_ref) * conv(2 * d + c0, p2_ref)).astype(BF16)


def _hyena_inproj(x2, mod_l, mod_index, seq_len, g, w_bf, b, conv_w, conv_b):
    t_rows, d = x2.shape
    tm = min(512, seq_len)
    halo = 16
    tps = seq_len // tm
    nblk = t_rows // halo
    cw_cols = 256
    kern = functools.partial(_inproj_kernel, tm=tm, tiles_per_seq=tps, halo=halo, cw_cols=cw_cols)
    return pl.pallas_call(
        kern,
        grid=(t_rows // tm,),
        in_specs=[pl.BlockSpec((tm, d), lambda t: (t, 0)),
                  pl.BlockSpec((halo, d), lambda t: (jnp.maximum(t * (tm // halo) - 1, 0), 0)),
                  pl.BlockSpec((halo, d), lambda t: (jnp.minimum((t + 1) * (tm // halo), nblk - 1), 0)),
                  pl.BlockSpec((None, MOD_CHUNKS, d), lambda t: (mod_index(t, tps), 0, 0)),
                  _full((1, d)), _full((d, 3 * d)), _full((1, 3 * d)),
                  _full((HY_SHORT, 3 * d)), _full((1, 3 * d))],
        out_specs=[pl.BlockSpec((tm, d), lambda t: (t, 0)),
                   pl.BlockSpec((tm, d), lambda t: (t, 0))],
        out_shape=[jax.ShapeDtypeStruct((t_rows, d), BF16),
                   jax.ShapeDtypeStruct((t_rows, d), BF16)],
        scratch_shapes=[pltpu.VMEM((tm + 2 * halo, cw_cols), F32)] * 3,
        compiler_params=_params(1),
        name="hyena_inproj",
    )(x2, x2, x2, mod_l, g.reshape(1, d), w_bf, b.reshape(1, 3 * d), conv_w, conv_b.reshape(1, 3 * d))


def _filter_hidden_kernel(z_ref, w1_ref, b1_ref, f1_ref, w2_ref, b2_ref, f2_ref, o_ref):
    h = jnp.sin(f1_ref[...] * (_dot3(z_ref[...], w1_ref[...]) + b1_ref[...]))
    o_ref[...] = jnp.sin(f2_ref[...] * (_dot3(h, w2_ref[...]) + b2_ref[...]))


def _filter_out_kernel(h_ref, w3_ref, dl_ref, o_ref, sum_ref, *, seq_len):
    i = pl.program_id(1)
    tr, td = o_ref.shape
    r = lax.broadcasted_iota(jnp.int32, (tr, td), 0) + i * tr
    pos = jnp.where(r < seq_len, r, 2 * seq_len - r)
    t = pos.astype(F32) / seq_len
    kc = _dot3(h_ref[...], w3_ref[...]) * jnp.exp(-t * dl_ref[...])
    kc = jnp.where(r == seq_len, 0.0, kc)
    o_ref[...] = kc

    @pl.when(i == 0)
    def _():
        sum_ref[...] = jnp.zeros_like(sum_ref)

    sum_ref[...] += jnp.sum(jnp.abs(kc), axis=0, keepdims=True)


def _filter_features(seq_len):
    f32 = jnp.float32
    pos_f = jnp.arange(seq_len, dtype=f32)
    pos_b = ((seq_len - jnp.arange(seq_len)) % seq_len).astype(f32)
    pos = jnp.concatenate([pos_f, pos_b])
    t = pos / seq_len
    bands = jnp.linspace(1e-4, HY_BANDS - 1, HY_BANDS, dtype=f32)
    ang = (2.0 * math.pi / seq_len) * pos[:, None] * bands[None, :]
    return jnp.concatenate([t[:, None], jnp.cos(ang), -jnp.sin(ang)], axis=-1)


def _hyena_filter(seq_len, d, f_w1, f_b1, f_fr1, f_w2, f_b2, f_fr2, f_w3):
    z = _filter_features(seq_len)
    emb = z.shape[1]
    hid = f_w1.shape[1]
    rows = 2 * seq_len
    tr = min(2048, seq_len)
    h2 = pl.pallas_call(
        _filter_hidden_kernel,
        grid=(rows // tr,),
        in_specs=[pl.BlockSpec((tr, emb), lambda i: (i, 0)),
                  _full((emb, hid)), _full((1, hid)), _full((1, hid)),
                  _full((hid, hid)), _full((1, hid)), _full((1, hid))],
        out_specs=pl.BlockSpec((tr, hid), lambda i: (i, 0)),
        out_shape=jax.ShapeDtypeStruct((rows, hid), F32),
        compiler_params=_params(1),
        name="hyena_filter_hidden",
    )(z, f_w1, f_b1.reshape(1, hid), f_fr1.reshape(1, hid), f_w2, f_b2.reshape(1, hid),
      f_fr2.reshape(1, hid))
    deltas = jnp.abs(jnp.linspace(HY_MIN_DECAY, HY_MAX_DECAY, d, dtype=F32)).reshape(1, d)
    td = 256
    nd = d // td
    tiles_per_half = seq_len // tr
    return pl.pallas_call(
        functools.partial(_filter_out_kernel, seq_len=seq_len),
        grid=(nd, rows // tr),
        in_specs=[pl.BlockSpec((tr, hid), lambda j, i: (i, 0)),
                  pl.BlockSpec((hid, td), lambda j, i: (0, (i // tiles_per_half) * nd + j)),
                  pl.BlockSpec((1, td), lambda j, i: (0, j))],
        out_specs=[pl.BlockSpec((tr, td), lambda j, i: (i, j)),
                   pl.BlockSpec((1, td), lambda j, i: (0, j))],
        out_shape=[jax.ShapeDtypeStruct((rows, d), F32), jax.ShapeDtypeStruct((1, d), F32)],
        compiler_params=_params(2),
        name="hyena_filter_out",
    )(h2, f_w3, deltas)


def _dft_tables(n1):
    n2 = FFT_N2
    n = n1 * n2
    h1 = n1 // 2
    k1 = np.concatenate([np.arange(0, h1 + 1), np.arange(1, h1)])
    is_im = np.concatenate([np.zeros(h1 + 1, bool), np.ones(h1 - 1, bool)])
    th = 2.0 * np.pi * np.outer(k1, np.arange(n1)) / n1
    f1 = np.where(is_im[:, None], -np.sin(th), np.cos(th))
    wgt = np.where((k1 == 0) | (k1 == h1), 1.0, 2.0)[:, None] / n
    f1inv = (np.where(is_im[:, None], -np.sin(th), np.cos(th)) * wgt).T
    m = np.arange(n2)
    k2 = np.arange(n2)

    def g(k1v):
        ph = 2.0 * np.pi * np.outer(k2 * n1 + k1v, m) / n
        return np.cos(ph), -np.sin(ph)

    mf = np.zeros((h1, 2 * n2, 2 * n2))
    for j in range(1, h1):
        gr, gi = g(j)
        mf[j] = np.block([[gr, -gi], [gi, gr]])
    g0r, g0i = g(0)
    ghr, ghi = g(h1)
    zz = np.zeros((n2, n2))
    m0 = np.block([[g0r, zz], [g0i, zz], [zz, ghr], [zz, ghi]])
    mfi = np.transpose(mf, (0, 2, 1))
    m0i = m0.T
    as32 = lambda a: jnp.asarray(a.astype(np.float32))
    return dict(f1=as32(f1), f1inv=as32(f1inv), mf=as32(mf), mfi=as32(mfi), m0=as32(m0), m0i=as32(m0i))


def _stage1_kernel(f_ref, x_ref, o_ref):
    xt = jnp.swapaxes(x_ref[...].astype(BF16), 0, 1)
    f = f_ref[...]
    out = jnp.stack([_dot(f, xt[t]).astype(o_ref.dtype) for t in range(xt.shape[0])], axis=0)
    o_ref[...] = jnp.swapaxes(out, 0, 1)


def _dft_stage1(f_bf, x4, out_dtype):
    bx, r, n2, d = x4.shape
    m = f_bf.shape[0]
    tt = (2 if x4.dtype == BF16 else 1) * BF16_SUBLANES
    return pl.pallas_call(
        _stage1_kernel,
        grid=(bx, n2 // tt),
        in_specs=[_full((m, r)), pl.BlockSpec((None, r, tt, d), lambda b, j: (b, 0, j, 0))],
        out_specs=pl.BlockSpec((None, m, tt, d), lambda b, j: (b, 0, j, 0)),
        out_shape=jax.ShapeDtypeStruct((bx, m, n2, d), out_dtype),
        compiler_params=_params(2),
        name="dft_stage1",
    )(f_bf, x4)


def _filter_spec_kernel(a_ref, nrm_ref, mf_ref, m0_ref, ks_ref, kh_ref):
    j = pl.program_id(0)
    n2 = FFT_N2
    a = a_ref[...].reshape(2 * n2, a_ref.shape[-1])
    nrm = nrm_ref[...]

    @pl.when(j == 0)
    def _():
        x = _dot(m0_ref[...], a) / nrm
        ks_ref[...] = x[0:2 * n2].reshape(ks_ref.shape)
        kh_ref[...] = x[2 * n2:4 * n2].reshape(kh_ref.shape)

    @pl.when(j > 0)
    def _():
        ks_ref[...] = (_dot(mf_ref[...], a) / nrm).reshape(ks_ref.shape)


def _filter_spectrum(ak, nrm, tabs):
    _, h1, n2, d = ak.shape
    return pl.pallas_call(
        _filter_spec_kernel,
        grid=(h1,),
        in_specs=[pl.BlockSpec((2, None, n2, d), lambda j: (0, j, 0, 0)),
                  _full((1, d)),
                  pl.BlockSpec((None, 2 * n2, 2 * n2), lambda j: (j, 0, 0)),
                  _full((4 * n2, 2 * n2))],
        out_specs=[pl.BlockSpec((None, 2, n2, d), lambda j: (j, 0, 0, 0)),
                   _full((2, n2, d))],
        out_shape=[jax.ShapeDtypeStruct((h1, 2, n2, d), F32),
                   jax.ShapeDtypeStruct((2, n2, d), F32)],
        compiler_params=_params(1),
        name="hyena_filter_spectrum",
    )(ak, nrm, tabs["mf"].astype(BF16), tabs["m0"].astype(BF16))


def _cmul(xr, xi, kr, ki):
    return xr * kr - xi * ki, xr * ki + xi * kr


def _stage2_kernel(a_ref, mf_ref, mfi_ref, m0_ref, m0i_ref, ks_ref, kh_ref, o_ref):
    j = pl.program_id(0)
    n2 = FFT_N2
    nb = a_ref.shape[0]
    d = a_ref.shape[-1]

    @pl.when(j == 0)
    def _():
        for b in range(nb):
            a = a_ref[b].reshape(2 * n2, d)
            x = _dot(m0_ref[...], a)
            y0r, y0i = _cmul(x[0:n2], x[n2:2 * n2], ks_ref[0], ks_ref[1])
            y1r, y1i = _cmul(x[2 * n2:3 * n2], x[3 * n2:4 * n2], kh_ref[0], kh_ref[1])
            y = jnp.concatenate([y0r, y0i, y1r, y1i], axis=0).astype(BF16)
            o_ref[b] = _dot(m0i_ref[...], y).reshape(2, n2, d).astype(o_ref.dtype)

    @pl.when(j > 0)
    def _():
        for b in range(nb):
            a = a_ref[b].reshape(2 * n2, d)
            x = _dot(mf_ref[...], a)
            yr, yi = _cmul(x[0:n2], x[n2:2 * n2], ks_ref[0], ks_ref[1])
            y = jnp.concatenate([yr, yi], axis=0).astype(BF16)
            o_ref[b] = _dot(mfi_ref[...], y).reshape(2, n2, d).astype(o_ref.dtype)


def _dft_stage2(a5, tabs, ks, kh):
    nb, _, h1, n2, d = a5.shape
    blk = pl.BlockSpec((nb, 2, None, n2, d), lambda j: (0, 0, j, 0, 0))
    return pl.pallas_call(
        _stage2_kernel,
        grid=(h1,),
        in_specs=[blk,
                  pl.BlockSpec((None, 2 * n2, 2 * n2), lambda j: (j, 0, 0)),
                  pl.BlockSpec((None, 2 * n2, 2 * n2), lambda j: (j, 0, 0)),
                  _full((4 * n2, 2 * n2)), _full((2 * n2, 4 * n2)),
                  pl.BlockSpec((None, 2, n2, d), lambda j: (j, 0, 0, 0)),
                  _full((2, n2, d))],
        out_specs=blk,
        out_shape=jax.ShapeDtypeStruct(a5.shape, BF16),
        compiler_params=_params(1),
        name="dft_stage2",
    )(a5, tabs["mf"].astype(BF16), tabs["mfi"].astype(BF16), tabs["m0"].astype(BF16),
      tabs["m0i"].astype(BF16), ks, kh)


def _long_conv(u, kc, nrm):
    b, seq_len, d = u.shape
    n2 = FFT_N2
    n1 = 2 * seq_len // n2
    h1 = n1 // 2
    tabs = _dft_tables(n1)
    f1 = tabs["f1"].astype(BF16)
    ak = _dft_stage1(f1, kc.reshape(1, n1, n2, d), BF16)
    ks, kh = _filter_spectrum(ak.reshape(2, h1, n2, d), nrm, tabs)
    a = _dft_stage1(f1[:, :h1], u.reshape(b, h1, n2, d), BF16)
    bv = _dft_stage2(a.reshape(b, 2, h1, n2, d), tabs, ks, kh)
    y = _dft_stage1(tabs["f1inv"][:h1].astype(BF16), bv.reshape(b, n1, n2, d), BF16)
    return y.reshape(b, seq_len, d)


def _ctxconv_kernel(u_ref, kc_ref, nrm_ref, cf_ref, ci_ref, o_ref, *, seq_len):
    nf = 2 * seq_len
    kh, kl = _split_bf16(kc_ref[...])
    ksp = (_dot(cf_ref[...], kh) + _dot(cf_ref[...], kl)) / nrm_ref[...]
    kr, ki = ksp[0:nf], ksp[nf:2 * nf]
    for b in range(u_ref.shape[0]):
        x = _dot(cf_ref[:, 0:seq_len], u_ref[b])
        yr, yi = _cmul(x[0:nf], x[nf:2 * nf], kr, ki)
        y = jnp.concatenate([yr, yi], axis=0).astype(BF16)
        o_ref[b] = _dot(ci_ref[...], y).astype(o_ref.dtype)


def _short_seq_conv(u, kc, nrm):
    b, seq_len, d = u.shape
    nf = 2 * seq_len
    ph = 2.0 * np.pi * np.outer(np.arange(nf), np.arange(nf)) / nf
    cf = jnp.asarray(np.concatenate([np.cos(ph), -np.sin(ph)], axis=0).astype(np.float32))
    ci = jnp.asarray((np.concatenate([np.cos(ph), -np.sin(ph)], axis=1)[:seq_len] / nf).astype(np.float32))
    return pl.pallas_call(
        functools.partial(_ctxconv_kernel, seq_len=seq_len),
        grid=(1,),
        in_specs=[_full((b, seq_len, d)), _full((nf, d)), _full((1, d)), _full((2 * nf, nf)),
                  _full((seq_len, 2 * nf))],
        out_specs=_full((b, seq_len, d)),
        out_shape=jax.ShapeDtypeStruct((b, seq_len, d), BF16),
        compiler_params=_params(1),
        name="hyena_ctx_conv",
    )(u, kc, nrm, cf.astype(BF16), ci.astype(BF16))


def _mlp_body(x, mod_ref, g_ref, w1_ref, w2_ref, ff_chunk):
    h = _norm_mod(x, g_ref[...], mod_ref[3:4, :], mod_ref[4:5, :]).astype(BF16)
    acc = jnp.zeros(x.shape, F32)
    for c0 in range(0, w1_ref.shape[1], ff_chunk):
        a = jnp.maximum(_dot(h, w1_ref[:, c0:c0 + ff_chunk]), 0.0)
        acc = acc + _dot((a * a).astype(BF16), w2_ref[c0:c0 + ff_chunk, :])
    return x + mod_ref[5:6, :] * acc


def _hy_out_mlp_kernel(x_ref, x0_ref, u_ref, yc_ref, mod_ref, skip_ref, w_ref, b_ref, g_ref, w1_ref, w2_ref,
                       o_ref, *, ff_chunk):
    u = u_ref[...].astype(F32)
    y = x0_ref[...].astype(F32) * (yc_ref[...].astype(F32) + u * skip_ref[...])
    dx = _dot(y.astype(BF16), w_ref[...]) + b_ref[...]
    x = x_ref[...] + mod_ref[2:3, :] * dx
    o_ref[...] = _mlp_body(x, mod_ref, g_ref, w1_ref, w2_ref, ff_chunk)


def _resident(shape):
    nd = len(shape)
    return pl.BlockSpec(shape, lambda *_: (0,) * nd, pipeline_mode=pl.Buffered(1))


MLP_FF_CHUNK = 1024


def _hyena_out_mlp(x2, x0, u, yc, mod_l, mod_index, seq_len, skip, w_bf, b, g, w1_bf, w2_bf):
    t_rows, d = x2.shape
    ff = w1_bf.shape[1]
    tm = min(512, seq_len)
    tps = seq_len // tm
    row = pl.BlockSpec((tm, d), lambda t: (t, 0))
    return pl.pallas_call(
        functools.partial(_hy_out_mlp_kernel, ff_chunk=MLP_FF_CHUNK),
        grid=(t_rows // tm,),
        in_specs=[row, row, row, row,
                  pl.BlockSpec((None, MOD_CHUNKS, d), lambda t: (mod_index(t, tps), 0, 0)),
                  _full((1, d)), _resident((d, d)), _full((1, d)),
                  _full((1, d)), _resident((d, ff)), _resident((ff, d))],
        out_specs=row,
        out_shape=jax.ShapeDtypeStruct((t_rows, d), F32),
        compiler_params=_params(1),
        name="hyena_out_mlp",
    )(x2, x0, u, yc, mod_l, skip.reshape(1, d), w_bf, b.reshape(1, d), g.reshape(1, d), w1_bf, w2_bf)


def _mlp_kernel(x_ref, mod_ref, g_ref, w1_ref, w2_ref, o_ref, *, ff_chunk):
    o_ref[...] = _mlp_body(x_ref[...], mod_ref, g_ref, w1_ref, w2_ref, ff_chunk)


def _mlp(x2, mod_l, mod_index, seq_len, g, w1_bf, w2_bf):
    t_rows, d = x2.shape
    ff = w1_bf.shape[1]
    tm = min(512, seq_len)
    tps = seq_len // tm
    row = pl.BlockSpec((tm, d), lambda t: (t, 0))
    return pl.pallas_call(
        functools.partial(_mlp_kernel, ff_chunk=MLP_FF_CHUNK),
        grid=(t_rows // tm,),
        in_specs=[row,
                  pl.BlockSpec((None, MOD_CHUNKS, d), lambda t: (mod_index(t, tps), 0, 0)),
                  _full((1, d)), _resident((d, ff)), _resident((ff, d))],
        out_specs=row,
        out_shape=jax.ShapeDtypeStruct((t_rows, d), F32),
        compiler_params=_params(1),
        name="mlp",
    )(x2, mod_l, g.reshape(1, d), w1_bf, w2_bf)


def _rope(v, cos, sin_signed, hi_half):
    partner = jnp.where(hi_half, pltpu.roll(v, 16, axis=1), pltpu.roll(v, LANES - 16, axis=1))
    return v * cos + partner * sin_signed


def _qkv_kernel(*refs, with_q, with_rope, q_width, kv_width):
    x_ref, mod_ref, g_ref, w_ref, b_ref, bd_ref, qn_ref, kn_ref = refs[:8]
    idx = 8
    if with_rope:
        cos_ref, sin_ref = refs[idx:idx + 2]
        idx += 2
    outs = refs[idx:]
    if with_q:
        q_ref, kt_ref, v_ref = outs
    else:
        kt_ref, v_ref = outs
    h = _norm_mod(x_ref[...], g_ref[...], mod_ref[0:1, :], mod_ref[1:2, :]).astype(BF16)
    tm = h.shape[0]
    if with_rope:
        cos = cos_ref[...]
        sin = sin_ref[...]
        hi_half = (lax.broadcasted_iota(jnp.int32, (tm, LANES), 1) & 16) != 0

    def head_norm(p, w, bd):
        ms = _dot((p * p).astype(BF16), bd)
        return p * lax.rsqrt(ms + EPS) * w

    def rotate(p):
        if not with_rope:
            return p
        return jnp.concatenate([_rope(p[:, c0:c0 + LANES], cos, sin, hi_half)
                                for c0 in range(0, p.shape[1], LANES)], axis=1)

    off = 0
    if with_q:
        q = _dot(h, w_ref[:, 0:q_width]) + b_ref[:, 0:q_width]
        q_ref[...] = rotate(head_norm(q, qn_ref[...], bd_ref[...]) * Q_SCALE_LOG2).astype(q_ref.dtype)
        off = q_width
    k = _dot(h, w_ref[:, off:off + kv_width]) + b_ref[:, off:off + kv_width]
    k = rotate(head_norm(k, kn_ref[...], bd_ref[0:kv_width, 0:kv_width]))
    kt_ref[...] = k.T.astype(kt_ref.dtype)
    v = _dot(h, w_ref[:, off + kv_width:off + 2 * kv_width]) + b_ref[:, off + kv_width:off + 2 * kv_width]
    v_ref[...] = v.astype(v_ref.dtype)


def _rope_tables(seq_len):
    rows = seq_len // GRID_W
    row = jnp.repeat(jnp.arange(rows, dtype=F32), GRID_W)
    col = jnp.tile(jnp.arange(GRID_W, dtype=F32), rows)
    inv = ROPE_BASE ** (-jnp.arange(ROPE_NFREQ, dtype=F32) / ROPE_NFREQ)
    ar = row[:, None] * inv[None, :]
    ac = col[:, None] * inv[None, :]
    cos = jnp.concatenate([jnp.cos(ar), jnp.cos(ar), jnp.cos(ac), jnp.cos(ac)], axis=-1)
    sin = jnp.concatenate([-jnp.sin(ar), jnp.sin(ar), -jnp.sin(ac), jnp.sin(ac)], axis=-1)
    return jnp.tile(cos, (1, LANES // HEAD_DIM)), jnp.tile(sin, (1, LANES // HEAD_DIM))


def _qkv_proj(x2, mod_l, mod_index, seq_len, g, w_bf, b, qn, kn, with_q, rope):
    t_rows, d = x2.shape
    q_width = N_HEADS * HEAD_DIM
    kv_width = N_KV_HEADS * HEAD_DIM
    width = w_bf.shape[1]
    tm = min(512, seq_len)
    tps = seq_len // tm
    bd = jnp.asarray(np.kron(np.eye(N_HEADS), np.full((HEAD_DIM, HEAD_DIM), 1.0 / HEAD_DIM)).astype(np.float32))
    row = lambda w: pl.BlockSpec((tm, w), lambda t: (t, 0))
    in_specs = [row(d),
                pl.BlockSpec((None, MOD_CHUNKS, d), lambda t: (mod_index(t, tps), 0, 0)),
                _full((1, d)), _full((d, width)), _full((1, width)), _full((q_width, q_width)),
                _full((1, q_width)), _full((1, kv_width))]
    args = [x2, mod_l, g.reshape(1, d), w_bf, b.reshape(1, width), bd.astype(BF16),
            jnp.tile(qn, N_HEADS).reshape(1, q_width), jnp.tile(kn, N_KV_HEADS).reshape(1, kv_width)]
    if rope is not None:
        in_specs += [pl.BlockSpec((tm, LANES), lambda t: (t % tps, 0))] * 2
        args += list(rope)
    out_specs = [pl.BlockSpec((kv_width, tm), lambda t: (0, t)), row(kv_width)]
    out_shape = [jax.ShapeDtypeStruct((kv_width, t_rows), BF16), jax.ShapeDtypeStruct((t_rows, kv_width), BF16)]
    if with_q:
        out_specs = [row(q_width)] + out_specs
        out_shape = [jax.ShapeDtypeStruct((t_rows, q_width), BF16)] + out_shape
    return pl.pallas_call(
        functools.partial(_qkv_kernel, with_q=with_q, with_rope=rope is not None,
                          q_width=q_width, kv_width=kv_width),
        grid=(t_rows // tm,),
        in_specs=in_specs, out_specs=out_specs, out_shape=out_shape,
        compiler_params=_params(1),
        name="qkv_proj",
    )(*args)


def _attn_kernel(sink_ref, x_ref, q_ref, ktp_ref, ktm_ref, ktn_ref, ktc_ref, vp_ref, vm_ref, vn_ref, vc_ref,
                 mod_ref, w_ref, b_ref, o_ref, att_ref, *, tq, n_qt):
    i = pl.program_id(1)
    lc = vc_ref.shape[0]
    nw = 3 * WINDOW
    nk = nw + lc
    n_rb = tq // WINDOW
    r = lax.broadcasted_iota(jnp.int32, (WINDOW, nk), 0)
    c = lax.broadcasted_iota(jnp.int32, (WINDOW, nk), 1)
    rel = c - WINDOW - r
    band = jnp.logical_and(rel >= -WINDOW, rel <= WINDOW)
    biases = []
    for rb in range(n_rb):
        ok = band
        if rb == 0:
            ok = jnp.logical_and(ok, jnp.logical_or(i > 0, c >= WINDOW))
        if rb == n_rb - 1:
            ok = jnp.logical_and(ok, jnp.logical_or(i < n_qt - 1, c < 2 * WINDOW))
        ok = jnp.logical_or(ok, c >= nw)
        biases.append(jnp.where(ok, 0.0, NEG_BIG))
    zeros = jnp.zeros((HEAD_DIM, nk), BF16)
    for hk in range(N_KV_HEADS):
        hs = slice(hk * HEAD_DIM, (hk + 1) * HEAD_DIM)
        ktw = jnp.concatenate([ktp_ref[hs, :], ktm_ref[hs, :], ktn_ref[hs, :]], axis=1)
        vw = jnp.concatenate([vp_ref[:, hs], vm_ref[:, hs], vn_ref[:, hs]], axis=0)
        ktc = ktc_ref[hs, :]
        vcx = vc_ref[:, hs]
        heads = [hk * GROUP + gq for gq in range(GROUP)]
        for rb in range(n_rb):
            rows = slice(rb * WINDOW, (rb + 1) * WINDOW)
            kk = jnp.concatenate([ktw[:, rb * WINDOW:rb * WINDOW + nw], ktc], axis=1)
            vv = jnp.concatenate([vw[rb * WINDOW:rb * WINDOW + nw], vcx], axis=0)
            kk_pair = (jnp.concatenate([kk, zeros], axis=0), jnp.concatenate([zeros, kk], axis=0))
            ss = [_dot(q_ref[rows, (hd // 2) * LANES:(hd // 2 + 1) * LANES], kk_pair[hd % 2]) + biases[rb]
                  for hd in heads]
            ms = [jnp.maximum(jnp.max(s, axis=-1, keepdims=True), sink_ref[hd]) for s, hd in zip(ss, heads)]
            ps = [jnp.exp2(s - m) for s, m in zip(ss, ms)]
            dens = [jnp.sum(p, axis=-1, keepdims=True) + jnp.exp2(sink_ref[hd] - m)
                    for p, m, hd in zip(ps, ms, heads)]
            for p, den, hd in zip(ps, dens, heads):
                o = _dot(p.astype(BF16), vv) / den
                att_ref[rows, hd * HEAD_DIM:(hd + 1) * HEAD_DIM] = o.astype(BF16)
    dx = _dot(att_ref[...], w_ref[...]) + b_ref[...]
    o_ref[...] = x_ref[...] + mod_ref[2:3, :] * dx


def _window_attention(x2, q, kt, v, ktc, vc, sink, mod_l, batch, seq_len, w_bf, b):
    t_rows, d = x2.shape
    kvw = v.shape[1]
    lc = vc.shape[0] // batch
    tq = min(512, seq_len)
    n_qt = seq_len // tq
    wpt = tq // WINDOW
    nwb = t_rows // WINDOW
    tile = lambda bi, i: bi * n_qt + i
    prev_blk = lambda bi, i: jnp.maximum(tile(bi, i) * wpt - 1, 0)
    next_blk = lambda bi, i: jnp.minimum((tile(bi, i) + 1) * wpt, nwb - 1)
    qrow = lambda w: pl.BlockSpec((tq, w), lambda bi, i: (tile(bi, i), 0))
    return pl.pallas_call(
        functools.partial(_attn_kernel, tq=tq, n_qt=n_qt),
        grid=(batch, n_qt),
        in_specs=[pl.BlockSpec(memory_space=pltpu.SMEM),
                  qrow(d), qrow(q.shape[1]),
                  pl.BlockSpec((kvw, WINDOW), lambda bi, i: (0, prev_blk(bi, i))),
                  pl.BlockSpec((kvw, tq), lambda bi, i: (0, tile(bi, i))),
                  pl.BlockSpec((kvw, WINDOW), lambda bi, i: (0, next_blk(bi, i))),
                  pl.BlockSpec((kvw, lc), lambda bi, i: (0, bi)),
                  pl.BlockSpec((WINDOW, kvw), lambda bi, i: (prev_blk(bi, i), 0)),
                  qrow(kvw),
                  pl.BlockSpec((WINDOW, kvw), lambda bi, i: (next_blk(bi, i), 0)),
                  pl.BlockSpec((lc, kvw), lambda bi, i: (bi, 0)),
                  pl.BlockSpec((None, MOD_CHUNKS, d), lambda bi, i: (bi, 0, 0)),
                  _resident((d, d)), pl.BlockSpec((1, d), lambda bi, i: (0, 0))],
        out_specs=qrow(d),
        out_shape=jax.ShapeDtypeStruct((t_rows, d), F32),
        scratch_shapes=[pltpu.VMEM((tq, d), BF16)],
        compiler_params=_params(2),
        name="window_attention",
    )(sink, x2, q, kt, kt, kt, ktc, v, v, v, vc, mod_l, w_bf, b.reshape(1, d))


def _latent_mod_index(t, tiles_per_seq):
    return t // tiles_per_seq


def kernel(x, c, ctx, c_ctx, mod_w, mod_b, norm1_w, norm2_w, mlp_w1, mlp_w2, hy_w_in, hy_b_in, hy_conv_w,
           hy_conv_b, hy_f_w1, hy_f_b1, hy_f_freq1, hy_f_w2, hy_f_b2, hy_f_freq2, hy_f_w3, hy_skip, hy_w_out,
           hy_b_out, at_w_qkv, at_b_qkv, at_q_norm, at_k_norm, at_sink, at_w_out, at_b_out):
    batch, seq_len, d = x.shape
    lc = ctx.shape[1]
    ctx_row = batch

    def ctx_mod_index(t, tiles_per_seq):
        return ctx_row

    cond_rows = -(-(batch + 1) // 8) * 8
    cc = jnp.zeros((cond_rows, d), F32).at[:batch].set(c).at[batch].set(c_ctx)
    mod = _modulation(cc, mod_w, mod_b)

    x2 = x.reshape(batch * seq_len, d)
    c2 = ctx.reshape(batch * lc, d)

    w_in = hy_w_in[0].astype(BF16)
    w_out = hy_w_out[0].astype(BF16)
    w1 = mlp_w1[0].astype(BF16)
    w2 = mlp_w2[0].astype(BF16)
    fpar = (hy_f_w1[0], hy_f_b1[0], hy_f_freq1[0], hy_f_w2[0], hy_f_b2[0], hy_f_freq2[0], hy_f_w3[0])
    def hyena_layer(tok, n_seq, idx_fn, conv):
        x0, u = _hyena_inproj(tok, mod[0], idx_fn, n_seq, norm1_w[0], w_in, hy_b_in[0], hy_conv_w[0],
                              hy_conv_b[0])
        kc, nrm = _hyena_filter(n_seq, d, *fpar)
        yc = conv(u.reshape(batch, n_seq, d), kc, nrm).reshape(batch * n_seq, d)
        return _hyena_out_mlp(tok, x0, u, yc, mod[0], idx_fn, n_seq, hy_skip[0], w_out, hy_b_out[0],
                              norm2_w[0], w1, w2)

    x2 = hyena_layer(x2, seq_len, _latent_mod_index, _long_conv)
    c2 = hyena_layer(c2, lc, ctx_mod_index, _short_seq_conv)

    w_qkv = at_w_qkv[0].astype(BF16)
    q_width = N_HEADS * HEAD_DIM
    rope = _rope_tables(seq_len)
    q, kt, v = _qkv_proj(x2, mod[1], _latent_mod_index, seq_len, norm1_w[1], w_qkv, at_b_qkv[0],
                         at_q_norm[0], at_k_norm[0], True, rope)
    kt_ctx, v_ctx = _qkv_proj(c2, mod[1], ctx_mod_index, lc, norm1_w[1], w_qkv[:, q_width:],
                              at_b_qkv[0][q_width:], at_q_norm[0], at_k_norm[0], False, None)
    x2 = _window_attention(x2, q, kt, v, kt_ctx, v_ctx, at_sink[0].astype(F32) * LOG2_E, mod[1], batch,
                           seq_len, at_w_out[0].astype(BF16), at_b_out[0])
    x2 = _mlp(x2, mod[1], _latent_mod_index, seq_len, norm2_w[1], mlp_w1[1].astype(BF16),
              mlp_w2[1].astype(BF16))
    return x2.reshape(batch, seq_len, d)
```

```python
import functools
import math

import numpy as np
import jax
import jax.numpy as jnp
from jax import lax
from jax.experimental import pallas as pl
from jax.experimental.pallas import tpu as pltpu

F32 = jnp.float32
BF16 = jnp.bfloat16

EPS = 1e-6
MOD_CHUNKS = 6
GRID_W = 64
HY_SHORT = 3
HY_BANDS = 16
HY_DECAY_TARGET = 1e-2
HY_MAX_DECAY = math.log(HY_DECAY_TARGET) / 0.3
HY_MIN_DECAY = math.log(HY_DECAY_TARGET) / 1.5
N_HEADS = 16
N_KV_HEADS = 4
GROUP = N_HEADS // N_KV_HEADS
HEAD_DIM = 64
WINDOW = 128
ROPE_BASE = 10000.0
ROPE_NFREQ = HEAD_DIM // 4
LOG2_E = math.log2(math.e)
Q_SCALE_LOG2 = HEAD_DIM ** -0.5 * LOG2_E

V7X_VMEM_BYTES = 64 * 1024 * 1024
V7X_MXU_DIM = 256
VMEM_LIMIT = V7X_VMEM_BYTES - 8 * 1024 * 1024
LANES = 128
BF16_SUBLANES = 16
FFT_N2 = 128
NEG_BIG = -1e30
ATT_QB = 256


def _params(n_axes, flags=None):
    return pltpu.CompilerParams(dimension_semantics=("arbitrary",) * n_axes,
                                vmem_limit_bytes=VMEM_LIMIT, flags=flags)


def _split_bf16(a):
    hi = a.astype(BF16)
    lo = (a - hi.astype(F32)).astype(BF16)
    return hi, lo


def _dot(a, b):
    return jnp.dot(a, b, preferred_element_type=F32)


def _dot3(a, b):
    ah, al = _split_bf16(a)
    bh, bl = _split_bf16(b)
    return _dot(ah, bh) + (_dot(ah, bl) + _dot(al, bh))


def _norm_mod(x, g, shift, scale):
    ms = jnp.mean(x * x, axis=-1, keepdims=True)
    return x * lax.rsqrt(ms + EPS) * (g * (1.0 + scale)) + shift


def _full(shape):
    nd = len(shape)
    return pl.BlockSpec(shape, lambda *_: (0,) * nd)


def _mod_kernel(c_ref, w_ref, b_ref, o_ref):
    c = c_ref[...]
    s = c * jax.nn.sigmoid(c)
    o_ref[...] = _dot3(s, w_ref[...]) + b_ref[...]


def _modulation(cc, mod_w, mod_b):
    depth, d, n = mod_w.shape
    r = cc.shape[0]
    tn = 1536
    out = pl.pallas_call(
        _mod_kernel,
        grid=(depth, n // tn),
        in_specs=[pl.BlockSpec((r, d), lambda i, j: (0, 0)),
                  pl.BlockSpec((None, d, tn), lambda i, j: (i, 0, j)),
                  pl.BlockSpec((None, 1, tn), lambda i, j: (i, 0, j))],
        out_specs=pl.BlockSpec((None, r, tn), lambda i, j: (i, 0, j)),
        out_shape=jax.ShapeDtypeStruct((depth, r, n), F32),
        compiler_params=_params(2),
        name="modulation",
    )(cc, mod_w, mod_b.reshape(depth, 1, n))
    return out.reshape(depth, r, MOD_CHUNKS, d)


def _inproj_kernel(x_ref, xp_ref, xn_ref, mod_ref, g_ref, w_ref, b_ref, cw_ref, cb_ref,
                   x0_ref, u_ref, p0_ref, p1_ref, p2_ref, *, tm, tiles_per_seq, halo, cw_cols):
    t = pl.program_id(0)
    first = (t % tiles_per_seq) == 0
    last = (t % tiles_per_seq) == tiles_per_seq - 1
    g = g_ref[...]
    shift = mod_ref[0:1, :]
    scale = mod_ref[1:2, :]

    def nm(v):
        return _norm_mod(v, g, shift, scale).astype(BF16)

    h = jnp.concatenate([nm(xp_ref[...]), nm(x_ref[...]), nm(xn_ref[...])], axis=0)
    d = x_ref.shape[1]
    row = lax.broadcasted_iota(jnp.int32, (halo, cw_cols), 0)
    kill_top = jnp.logical_and(first, row == halo - 1)
    kill_bot = jnp.logical_and(last, row == 0)

    def conv(col0, p_ref):
        p = _dot(h, w_ref[:, col0:col0 + cw_cols]) + b_ref[:, col0:col0 + cw_cols]
        p_ref[0:halo, :] = jnp.where(kill_top, 0.0, p[0:halo])
        p_ref[halo:tm + halo, :] = p[halo:tm + halo]
        p_ref[tm + halo:tm + 2 * halo, :] = jnp.where(kill_bot, 0.0, p[tm + halo:tm + 2 * halo])
        out = cb_ref[:, col0:col0 + cw_cols]
        for k in range(HY_SHORT):
            out = out + p_ref[halo - 1 + k:halo - 1 + k + tm, :] * cw_ref[k:k + 1, col0:col0 + cw_cols]
        return out

    for c0 in range(0, d, cw_cols):
        x0_ref[:, c0:c0 + cw_cols] = conv(c0, p0_ref).astype(BF16)
        u_ref[:, c0:c0 + cw_cols] = (conv(d + c0, p1_ref) * conv(2 * d + c0, p2_ref)).astype(BF16)


def _hyena_inproj(x2, mod_l, mod_index, seq_len, g, w_bf, b, conv_w, conv_b):
    t_rows, d = x2.shape
    tm = min(512, seq_len)
    halo = 16
    tps = seq_len // tm
    nblk = t_rows // halo
    cw_cols = 256
    kern = functools.partial(_inproj_kernel, tm=tm, tiles_per_seq=tps, halo=halo, cw_cols=cw_cols)
    return pl.pallas_call(
        kern,
        grid=(t_rows // tm,),
        in_specs=[pl.BlockSpec((tm, d), lambda t: (t, 0)),
                  pl.BlockSpec((halo, d), lambda t: (jnp.maximum(t * (tm // halo) - 1, 0), 0)),
                  pl.BlockSpec((halo, d), lambda t: (jnp.minimum((t + 1) * (tm // halo), nblk - 1), 0)),
                  pl.BlockSpec((None, MOD_CHUNKS, d), lambda t: (mod_index(t, tps), 0, 0)),
                  _full((1, d)), _full((d, 3 * d)), _full((1, 3 * d)),
                  _full((HY_SHORT, 3 * d)), _full((1, 3 * d))],
        out_specs=[pl.BlockSpec((tm, d), lambda t: (t, 0)),
                   pl.BlockSpec((tm, d), lambda t: (t, 0))],
        out_shape=[jax.ShapeDtypeStruct((t_rows, d), BF16),
                   jax.ShapeDtypeStruct((t_rows, d), BF16)],
        scratch_shapes=[pltpu.VMEM((tm + 2 * halo, cw_cols), F32)] * 3,
        compiler_params=_params(1),
        name="hyena_inproj",
    )(x2, x2, x2, mod_l, g.reshape(1, d), w_bf, b.reshape(1, 3 * d), conv_w, conv_b.reshape(1, 3 * d))


def _filter_hidden_kernel(z_ref, w1_ref, b1_ref, f1_ref, w2_ref, b2_ref, f2_ref, o_ref):
    h = jnp.sin(f1_ref[...] * (_dot3(z_ref[...], w1_ref[...]) + b1_ref[...]))
    o_ref[...] = jnp.sin(f2_ref[...] * (_dot3(h, w2_ref[...]) + b2_ref[...]))


def _filter_out_kernel(h_ref, w3_ref, dl_ref, o_ref, sum_ref, *, seq_len):
    i = pl.program_id(1)
    tr, td = o_ref.shape
    r = lax.broadcasted_iota(jnp.int32, (tr, td), 0) + i * tr
    pos = jnp.where(r < seq_len, r, 2 * seq_len - r)
    t = pos.astype(F32) / seq_len
    kc = _dot3(h_ref[...], w3_ref[...]) * jnp.exp(-t * dl_ref[...])
    kc = jnp.where(r == seq_len, 0.0, kc)
    o_ref[...] = kc

    @pl.when(i == 0)
    def _():
        sum_ref[...] = jnp.zeros_like(sum_ref)

    sum_ref[...] += jnp.sum(jnp.abs(kc), axis=0, keepdims=True)


def _filter_features(seq_len):
    f32 = jnp.float32
    pos_f = jnp.arange(seq_len, dtype=f32)
    pos_b = ((seq_len - jnp.arange(seq_len)) % seq_len).astype(f32)
    pos = jnp.concatenate([pos_f, pos_b])
    t = pos / seq_len
    bands = jnp.linspace(1e-4, HY_BANDS - 1, HY_BANDS, dtype=f32)
    ang = (2.0 * math.pi / seq_len) * pos[:, None] * bands[None, :]
    return jnp.concatenate([t[:, None], jnp.cos(ang), -jnp.sin(ang)], axis=-1)


def _hyena_filter(seq_len, d, f_w1, f_b1, f_fr1, f_w2, f_b2, f_fr2, f_w3):
    z = _filter_features(seq_len)
    emb = z.shape[1]
    hid = f_w1.shape[1]
    rows = 2 * seq_len
    tr = min(2048, seq_len)
    h2 = pl.pallas_call(
        _filter_hidden_kernel,
        grid=(rows // tr,),
        in_specs=[pl.BlockSpec((tr, emb), lambda i: (i, 0)),
                  _full((emb, hid)), _full((1, hid)), _full((1, hid)),
                  _full((hid, hid)), _full((1, hid)), _full((1, hid))],
        out_specs=pl.BlockSpec((tr, hid), lambda i: (i, 0)),
        out_shape=jax.ShapeDtypeStruct((rows, hid), F32),
        compiler_params=_params(1),
        name="hyena_filter_hidden",
    )(z, f_w1, f_b1.reshape(1, hid), f_fr1.reshape(1, hid), f_w2, f_b2.reshape(1, hid),
      f_fr2.reshape(1, hid))
    deltas = jnp.abs(jnp.linspace(HY_MIN_DECAY, HY_MAX_DECAY, d, dtype=F32)).reshape(1, d)
    td = 256
    nd = d // td
    tiles_per_half = seq_len // tr
    return pl.pallas_call(
        functools.partial(_filter_out_kernel, seq_len=seq_len),
        grid=(nd, rows // tr),
        in_specs=[pl.BlockSpec((tr, hid), lambda j, i: (i, 0)),
                  pl.BlockSpec((hid, td), lambda j, i: (0, (i // tiles_per_half) * nd + j)),
                  pl.BlockSpec((1, td), lambda j, i: (0, j))],
        out_specs=[pl.BlockSpec((tr, td), lambda j, i: (i, j)),
                   pl.BlockSpec((1, td), lambda j, i: (0, j))],
        out_shape=[jax.ShapeDtypeStruct((rows, d), F32), jax.ShapeDtypeStruct((1, d), F32)],
        compiler_params=_params(2),
        name="hyena_filter_out",
    )(h2, f_w3, deltas)


def _dft_tables(n1):
    n2 = FFT_N2
    n = n1 * n2
    h1 = n1 // 2
    k1 = np.concatenate([np.arange(0, h1 + 1), np.arange(1, h1)])
    is_im = np.concatenate([np.zeros(h1 + 1, bool), np.ones(h1 - 1, bool)])
    th = 2.0 * np.pi * np.outer(k1, np.arange(n1)) / n1
    f1 = np.where(is_im[:, None], -np.sin(th), np.cos(th))
    wgt = np.where((k1 == 0) | (k1 == h1), 1.0, 2.0)[:, None] / n
    f1inv = (np.where(is_im[:, None], -np.sin(th), np.cos(th)) * wgt).T
    m = np.arange(n2)
    k2 = np.arange(n2)

    def g(k1v):
        ph = 2.0 * np.pi * np.outer(k2 * n1 + k1v, m) / n
        return np.cos(ph), -np.sin(ph)

    mf = np.zeros((h1, 2 * n2, 2 * n2))
    for j in range(1, h1):
        gr, gi = g(j)
        mf[j] = np.block([[gr, -gi], [gi, gr]])
    g0r, g0i = g(0)
    ghr, ghi = g(h1)
    zz = np.zeros((n2, n2))
    m0 = np.block([[g0r, zz], [g0i, zz], [zz, ghr], [zz, ghi]])
    mfi = np.transpose(mf, (0, 2, 1))
    m0i = m0.T
    as32 = lambda a: jnp.asarray(a.astype(np.float32))
    return dict(f1=as32(f1), f1inv=as32(f1inv), mf=as32(mf), mfi=as32(mfi), m0=as32(m0), m0i=as32(m0i))


def _stage1_kernel(f_ref, x_ref, o_ref):
    xt = jnp.swapaxes(x_ref[...].astype(BF16), 0, 1)
    f = f_ref[...]
    out = jnp.stack([_dot(f, xt[t]).astype(o_ref.dtype) for t in range(xt.shape[0])], axis=0)
    o_ref[...] = jnp.swapaxes(out, 0, 1)


def _dft_stage1(f_bf, x4, out_dtype):
    bx, r, n2, d = x4.shape
    m = f_bf.shape[0]
    tt = (2 if x4.dtype == BF16 else 1) * BF16_SUBLANES
    return pl.pallas_call(
        _stage1_kernel,
        grid=(bx, n2 // tt),
        in_specs=[_full((m, r)), pl.BlockSpec((None, r, tt, d), lambda b, j: (b, 0, j, 0))],
        out_specs=pl.BlockSpec((None, m, tt, d), lambda b, j: (b, 0, j, 0)),
        out_shape=jax.ShapeDtypeStruct((bx, m, n2, d), out_dtype),
        compiler_params=_params(2),
        name="dft_stage1",
    )(f_bf, x4)


def _filter_spec_kernel(a_ref, nrm_ref, mf_ref, m0_ref, ks_ref, kh_ref):
    j = pl.program_id(0)
    n2 = FFT_N2
    a = a_ref[...].reshape(2 * n2, a_ref.shape[-1])
    nrm = nrm_ref[...]

    @pl.when(j == 0)
    def _():
        x = _dot(m0_ref[...], a) / nrm
        ks_ref[...] = x[0:2 * n2].reshape(ks_ref.shape)
        kh_ref[...] = x[2 * n2:4 * n2].reshape(kh_ref.shape)

    @pl.when(j > 0)
    def _():
        ks_ref[...] = (_dot(mf_ref[...], a) / nrm).reshape(ks_ref.shape)


def _filter_spectrum(ak, nrm, tabs):
    _, h1, n2, d = ak.shape
    return pl.pallas_call(
        _filter_spec_kernel,
        grid=(h1,),
        in_specs=[pl.BlockSpec((2, None, n2, d), lambda j: (0, j, 0, 0)),
                  _full((1, d)),
                  pl.BlockSpec((None, 2 * n2, 2 * n2), lambda j: (j, 0, 0)),
                  _full((4 * n2, 2 * n2))],
        out_specs=[pl.BlockSpec((None, 2, n2, d), lambda j: (j, 0, 0, 0)),
                   _full((2, n2, d))],
        out_shape=[jax.ShapeDtypeStruct((h1, 2, n2, d), F32),
                   jax.ShapeDtypeStruct((2, n2, d), F32)],
        compiler_params=_params(1),
        name="hyena_filter_spectrum",
    )(ak, nrm, tabs["mf"].astype(BF16), tabs["m0"].astype(BF16))


def _cmul(xr, xi, kr, ki):
    return xr * kr - xi * ki, xr * ki + xi * kr


def _stage2_kernel(a_ref, mf_ref, mfi_ref, m0_ref, m0i_ref, ks_ref, kh_ref, o_ref):
    j = pl.program_id(0)
    n2 = FFT_N2
    nb = a_ref.shape[0]
    d = a_ref.shape[-1]

    @pl.when(j == 0)
    def _():
        for b in range(nb):
            a = a_ref[b].reshape(2 * n2, d)
            x = _dot(m0_ref[...], a)
            y0r, y0i = _cmul(x[0:n2], x[n2:2 * n2], ks_ref[0], ks_ref[1])
            y1r, y1i = _cmul(x[2 * n2:3 * n2], x[3 * n2:4 * n2], kh_ref[0], kh_ref[1])
            y = jnp.concatenate([y0r, y0i, y1r, y1i], axis=0).astype(BF16)
            o_ref[b] = _dot(m0i_ref[...], y).reshape(2, n2, d).astype(o_ref.dtype)

    @pl.when(j > 0)
    def _():
        for b in range(nb):
            a = a_ref[b].reshape(2 * n2, d)
            x = _dot(mf_ref[...], a)
            yr, yi = _cmul(x[0:n2], x[n2:2 * n2], ks_ref[0], ks_ref[1])
            y = jnp.concatenate([yr, yi], axis=0).astype(BF16)
            o_ref[b] = _dot(mfi_ref[...], y).reshape(2, n2, d).astype(o_ref.dtype)


def _dft_stage2(a5, tabs, ks, kh):
    nb, _, h1, n2, d = a5.shape
    blk = pl.BlockSpec((nb, 2, None, n2, d), lambda j: (0, 0, j, 0, 0))
    return pl.pallas_call(
        _stage2_kernel,
        grid=(h1,),
        in_specs=[blk,
                  pl.BlockSpec((None, 2 * n2, 2 * n2), lambda j: (j, 0, 0)),
                  pl.BlockSpec((None, 2 * n2, 2 * n2), lambda j: (j, 0, 0)),
                  _full((4 * n2, 2 * n2)), _full((2 * n2, 4 * n2)),
                  pl.BlockSpec((None, 2, n2, d), lambda j: (j, 0, 0, 0)),
                  _full((2, n2, d))],
        out_specs=blk,
        out_shape=jax.ShapeDtypeStruct(a5.shape, BF16),
        compiler_params=_params(1),
        name="dft_stage2",
    )(a5, tabs["mf"].astype(BF16), tabs["mfi"].astype(BF16), tabs["m0"].astype(BF16),
      tabs["m0i"].astype(BF16), ks, kh)


def _long_conv(u, kc, nrm):
    b, seq_len, d = u.shape
    n2 = FFT_N2
    n1 = 2 * seq_len // n2
    h1 = n1 // 2
    tabs = _dft_tables(n1)
    f1 = tabs["f1"].astype(BF16)
    ak = _dft_stage1(f1, kc.reshape(1, n1, n2, d), BF16)
    ks, kh = _filter_spectrum(ak.reshape(2, h1, n2, d), nrm, tabs)
    a = _dft_stage1(f1[:, :h1], u.reshape(b, h1, n2, d), BF16)
    bv = _dft_stage2(a.reshape(b, 2, h1, n2, d), tabs, ks, kh)
    y = _dft_stage1(tabs["f1inv"][:h1].astype(BF16), bv.reshape(b, n1, n2, d), BF16)
    return y.reshape(b, seq_len, d)


def _ctxconv_kernel(u_ref, kc_ref, nrm_ref, cf_ref, ci_ref, o_ref, *, seq_len):
    nf = 2 * seq_len
    kh, kl = _split_bf16(kc_ref[...])
    ksp = (_dot(cf_ref[...], kh) + _dot(cf_ref[...], kl)) / nrm_ref[...]
    kr, ki = ksp[0:nf], ksp[nf:2 * nf]
    for b in range(u_ref.shape[0]):
        x = _dot(cf_ref[:, 0:seq_len], u_ref[b])
        yr, yi = _cmul(x[0:nf], x[nf:2 * nf], kr, ki)
        y = jnp.concatenate([yr, yi], axis=0).astype(BF16)
        o_ref[b] = _dot(ci_ref[...], y).astype(o_ref.dtype)


def _short_seq_conv(u, kc, nrm):
    b, seq_len, d = u.shape
    nf = 2 * seq_len
    ph = 2.0 * np.pi * np.outer(np.arange(nf), np.arange(nf)) / nf
    cf = jnp.asarray(np.concatenate([np.cos(ph), -np.sin(ph)], axis=0).astype(np.float32))
    ci = jnp.asarray((np.concatenate([np.cos(ph), -np.sin(ph)], axis=1)[:seq_len] / nf).astype(np.float32))
    return pl.pallas_call(
        functools.partial(_ctxconv_kernel, seq_len=seq_len),
        grid=(1,),
        in_specs=[_full((b, seq_len, d)), _full((nf, d)), _full((1, d)), _full((2 * nf, nf)),
                  _full((seq_len, 2 * nf))],
        out_specs=_full((b, seq_len, d)),
        out_shape=jax.ShapeDtypeStruct((b, seq_len, d), BF16),
        compiler_params=_params(1),
        name="hyena_ctx_conv",
    )(u, kc, nrm, cf.astype(BF16), ci.astype(BF16))


def _mlp_body(x, mod_ref, g_ref, w1_ref, w2_ref, ff_chunk):
    h = _norm_mod(x, g_ref[...], mod_ref[3:4, :], mod_ref[4:5, :]).astype(BF16)
    acc = jnp.zeros(x.shape, F32)
    for c0 in range(0, w1_ref.shape[1], ff_chunk):
        a = jnp.maximum(_dot(h, w1_ref[:, c0:c0 + ff_chunk]), 0.0)
        acc = acc + _dot((a * a).astype(BF16), w2_ref[c0:c0 + ff_chunk, :])
    return x + mod_ref[5:6, :] * acc


def _hy_out_mlp_kernel(x_ref, x0_ref, u_ref, yc_ref, mod_ref, skip_ref, w_ref, b_ref, g_ref, w1_ref, w2_ref,
                       o_ref, *, ff_chunk):
    u = u_ref[...].astype(F32)
    y = x0_ref[...].astype(F32) * (yc_ref[...].astype(F32) + u * skip_ref[...])
    dx = _dot(y.astype(BF16), w_ref[...]) + b_ref[...]
    x = x_ref[...] + mod_ref[2:3, :] * dx
    o_ref[...] = _mlp_body(x, mod_ref, g_ref, w1_ref, w2_ref, ff_chunk)


def _resident(shape):
    nd = len(shape)
    return pl.BlockSpec(shape, lambda *_: (0,) * nd, pipeline_mode=pl.Buffered(1))


MLP_FF_CHUNK = 1024


def _hyena_out_mlp(x2, x0, u, yc, mod_l, mod_index, seq_len, skip, w_bf, b, g, w1_bf, w2_bf):
    t_rows, d = x2.shape
    ff = w1_bf.shape[1]
    tm = min(512, seq_len)
    tps = seq_len // tm
    row = pl.BlockSpec((tm, d), lambda t: (t, 0))
    return pl.pallas_call(
        functools.partial(_hy_out_mlp_kernel, ff_chunk=MLP_FF_CHUNK),
        grid=(t_rows // tm,),
        in_specs=[row, row, row, row,
                  pl.BlockSpec((None, MOD_CHUNKS, d), lambda t: (mod_index(t, tps), 0, 0)),
                  _full((1, d)), _resident((d, d)), _full((1, d)),
                  _full((1, d)), _resident((d, ff)), _resident((ff, d))],
        out_specs=row,
        out_shape=jax.ShapeDtypeStruct((t_rows, d), F32),
        compiler_params=_params(1),
        name="hyena_out_mlp",
    )(x2, x0, u, yc, mod_l, skip.reshape(1, d), w_bf, b.reshape(1, d), g.reshape(1, d), w1_bf, w2_bf)


def _mlp_kernel(x_ref, mod_ref, g_ref, w1_ref, w2_ref, o_ref, *, ff_chunk):
    o_ref[...] = _mlp_body(x_ref[...], mod_ref, g_ref, w1_ref, w2_ref, ff_chunk)


def _mlp(x2, mod_l, mod_index, seq_len, g, w1_bf, w2_bf):
    t_rows, d = x2.shape
    ff = w1_bf.shape[1]
    tm = min(512, seq_len)
    tps = seq_len // tm
    row = pl.BlockSpec((tm, d), lambda t: (t, 0))
    return pl.pallas_call(
        functools.partial(_mlp_kernel, ff_chunk=MLP_FF_CHUNK),
        grid=(t_rows // tm,),
        in_specs=[row,
                  pl.BlockSpec((None, MOD_CHUNKS, d), lambda t: (mod_index(t, tps), 0, 0)),
                  _full((1, d)), _resident((d, ff)), _resident((ff, d))],
        out_specs=row,
        out_shape=jax.ShapeDtypeStruct((t_rows, d), F32),
        compiler_params=_params(1),
        name="mlp",
    )(x2, mod_l, g.reshape(1, d), w1_bf, w2_bf)


def _rope(v, cos, sin_signed, hi_half):
    partner = jnp.where(hi_half, pltpu.roll(v, 16, axis=1), pltpu.roll(v, LANES - 16, axis=1))
    return v * cos + partner * sin_signed


def _qkv_kernel(*refs, with_q, with_rope, q_width, kv_width):
    x_ref, mod_ref, g_ref, w_ref, b_ref, bd_ref, qn_ref, kn_ref = refs[:8]
    idx = 8
    if with_rope:
        cos_ref, sin_ref = refs[idx:idx + 2]
        idx += 2
    outs = refs[idx:]
    if with_q:
        q_ref, kt_ref, v_ref = outs
    else:
        kt_ref, v_ref = outs
    h = _norm_mod(x_ref[...], g_ref[...], mod_ref[0:1, :], mod_ref[1:2, :]).astype(BF16)
    tm = h.shape[0]
    if with_rope:
        cos = cos_ref[...]
        sin = sin_ref[...]
        hi_half = (lax.broadcasted_iota(jnp.int32, (tm, LANES), 1) & 16) != 0

    def head_norm(p, w, bd):
        sq = (p * p).astype(BF16)
        sw = bd.shape[0]
        ms = jnp.concatenate([_dot(sq[:, c0:c0 + sw], bd) for c0 in range(0, p.shape[1], sw)], axis=1)
        return p * lax.rsqrt(ms + EPS) * w

    def rotate(p):
        if not with_rope:
            return p
        return jnp.concatenate([_rope(p[:, c0:c0 + LANES], cos, sin, hi_half)
                                for c0 in range(0, p.shape[1], LANES)], axis=1)

    off = 0
    if with_q:
        q = _dot(h, w_ref[:, 0:q_width]) + b_ref[:, 0:q_width]
        q_ref[...] = rotate(head_norm(q, qn_ref[...], bd_ref[...]) * Q_SCALE_LOG2).astype(q_ref.dtype)
        off = q_width
    k = _dot(h, w_ref[:, off:off + kv_width]) + b_ref[:, off:off + kv_width]
    k = rotate(head_norm(k, kn_ref[...], bd_ref[...]))
    kt_ref[...] = k.T.astype(kt_ref.dtype)
    v = _dot(h, w_ref[:, off + kv_width:off + 2 * kv_width]) + b_ref[:, off + kv_width:off + 2 * kv_width]
    v_ref[...] = v.astype(v_ref.dtype)


def _rope_tables(seq_len):
    rows = seq_len // GRID_W
    row = jnp.repeat(jnp.arange(rows, dtype=F32), GRID_W)
    col = jnp.tile(jnp.arange(GRID_W, dtype=F32), rows)
    inv = ROPE_BASE ** (-jnp.arange(ROPE_NFREQ, dtype=F32) / ROPE_NFREQ)
    ar = row[:, None] * inv[None, :]
    ac = col[:, None] * inv[None, :]
    cos = jnp.concatenate([jnp.cos(ar), jnp.cos(ar), jnp.cos(ac), jnp.cos(ac)], axis=-1)
    sin = jnp.concatenate([-jnp.sin(ar), jnp.sin(ar), -jnp.sin(ac), jnp.sin(ac)], axis=-1)
    return jnp.tile(cos, (1, LANES // HEAD_DIM)), jnp.tile(sin, (1, LANES // HEAD_DIM))


def _qkv_proj(x2, mod_l, mod_index, seq_len, g, w_bf, b, qn, kn, with_q, rope):
    t_rows, d = x2.shape
    q_width = N_HEADS * HEAD_DIM
    kv_width = N_KV_HEADS * HEAD_DIM
    width = w_bf.shape[1]
    tm = min(512, seq_len)
    tps = seq_len // tm
    slab_heads = V7X_MXU_DIM // HEAD_DIM
    bd = jnp.asarray(np.kron(np.eye(slab_heads), np.full((HEAD_DIM, HEAD_DIM), 1.0 / HEAD_DIM)).astype(np.float32))
    row = lambda w: pl.BlockSpec((tm, w), lambda t: (t, 0))
    in_specs = [row(d),
                pl.BlockSpec((None, MOD_CHUNKS, d), lambda t: (mod_index(t, tps), 0, 0)),
                _full((1, d)), _full((d, width)), _full((1, width)), _full((V7X_MXU_DIM, V7X_MXU_DIM)),
                _full((1, q_width)), _full((1, kv_width))]
    args = [x2, mod_l, g.reshape(1, d), w_bf, b.reshape(1, width), bd.astype(BF16),
            jnp.tile(qn, N_HEADS).reshape(1, q_width), jnp.tile(kn, N_KV_HEADS).reshape(1, kv_width)]
    if rope is not None:
        in_specs += [pl.BlockSpec((tm, LANES), lambda t: (t % tps, 0))] * 2
        args += list(rope)
    out_specs = [pl.BlockSpec((kv_width, tm), lambda t: (0, t)), row(kv_width)]
    out_shape = [jax.ShapeDtypeStruct((kv_width, t_rows), BF16), jax.ShapeDtypeStruct((t_rows, kv_width), BF16)]
    if with_q:
        out_specs = [row(q_width)] + out_specs
        out_shape = [jax.ShapeDtypeStruct((t_rows, q_width), BF16)] + out_shape
    return pl.pallas_call(
        functools.partial(_qkv_kernel, with_q=with_q, with_rope=rope is not None,
                          q_width=q_width, kv_width=kv_width),
        grid=(t_rows // tm,),
        in_specs=in_specs, out_specs=out_specs, out_shape=out_shape,
        compiler_params=_params(1),
        name="qkv_proj",
    )(*args)


def _attn_kernel(sink_ref, x_ref, q_ref, ktp_ref, ktm_ref, ktn_ref, ktc_ref, vp_ref, vm_ref, vn_ref, vc_ref,
                 mod_ref, w_ref, b_ref, o_ref, att_ref, *, tq, n_qt):
    i = pl.program_id(1)
    lc = vc_ref.shape[0]
    qb = min(ATT_QB, tq)
    n_sub = qb // WINDOW
    nw = qb + 2 * WINDOW
    nk = nw + lc
    sw = 3 * WINDOW
    sk = sw + lc
    n_rb = tq // qb
    r = lax.broadcasted_iota(jnp.int32, (WINDOW, sk), 0)
    c = lax.broadcasted_iota(jnp.int32, (WINDOW, sk), 1)
    rel = c - WINDOW - r
    band = jnp.logical_and(rel >= -WINDOW, rel <= WINDOW)
    first_ok = jnp.logical_and(band, jnp.logical_or(i > 0, c >= WINDOW))
    last_ok = jnp.logical_and(band, jnp.logical_or(i < n_qt - 1, c < 2 * WINDOW))
    mid_bias = jnp.where(jnp.logical_or(band, c >= sw), 0.0, NEG_BIG)
    first_bias = jnp.where(jnp.logical_or(first_ok, c >= sw), 0.0, NEG_BIG)
    last_bias = jnp.where(jnp.logical_or(last_ok, c >= sw), 0.0, NEG_BIG)

    def sub_bias(rb, j):
        if rb == 0 and j == 0:
            return first_bias
        if rb == n_rb - 1 and j == n_sub - 1:
            return last_bias
        return mid_bias

    zeros = jnp.zeros((HEAD_DIM, nk), BF16)
    for hk in range(N_KV_HEADS):
        hs = slice(hk * HEAD_DIM, (hk + 1) * HEAD_DIM)
        ktw = jnp.concatenate([ktp_ref[hs, :], ktm_ref[hs, :], ktn_ref[hs, :]], axis=1)
        vw = jnp.concatenate([vp_ref[:, hs], vm_ref[:, hs], vn_ref[:, hs]], axis=0)
        ktc = ktc_ref[hs, :]
        vcx = vc_ref[:, hs]
        heads = [hk * GROUP + gq for gq in range(GROUP)]
        for rb in range(n_rb):
            rows = slice(rb * qb, (rb + 1) * qb)
            kk = jnp.concatenate([ktw[:, rb * qb:rb * qb + nw], ktc], axis=1)
            vv = jnp.concatenate([vw[rb * qb:rb * qb + nw], vcx], axis=0)
            kk_pair = (jnp.concatenate([kk, zeros], axis=0), jnp.concatenate([zeros, kk], axis=0))
            raw = [_dot(q_ref[rows, (hd // 2) * LANES:(hd // 2 + 1) * LANES], kk_pair[hd % 2]) for hd in heads]
            chains = [(gq, j) for gq in range(GROUP) for j in range(n_sub)]
            ss = [jnp.concatenate([raw[gq][j * WINDOW:(j + 1) * WINDOW, j * WINDOW:j * WINDOW + sw],
                                   raw[gq][j * WINDOW:(j + 1) * WINDOW, nw:nk]], axis=1) + sub_bias(rb, j)
                  for gq, j in chains]
            ms = [jnp.maximum(jnp.max(s, axis=-1, keepdims=True), sink_ref[heads[gq]])
                  for s, (gq, j) in zip(ss, chains)]
            ps = [jnp.exp2(s - m) for s, m in zip(ss, ms)]
            dens = [jnp.sum(p, axis=-1, keepdims=True) + jnp.exp2(sink_ref[heads[gq]] - m)
                    for p, m, (gq, j) in zip(ps, ms, chains)]
            for gq, hd in enumerate(heads):
                full = []
                for j in range(n_sub):
                    p = ps[gq * n_sub + j].astype(BF16)
                    parts = [p[:, 0:sw], p[:, sw:sk]]
                    if j > 0:
                        parts.insert(0, jnp.zeros((WINDOW, j * WINDOW), BF16))
                    if nw - sw - j * WINDOW > 0:
                        parts.insert(-1, jnp.zeros((WINDOW, nw - sw - j * WINDOW), BF16))
                    full.append(jnp.concatenate(parts, axis=1))
                den = jnp.concatenate(dens[gq * n_sub:(gq + 1) * n_sub], axis=0)
                o = _dot(jnp.concatenate(full, axis=0), vv) / den
                att_ref[rows, hd * HEAD_DIM:(hd + 1) * HEAD_DIM] = o.astype(BF16)
    dx = _dot(att_ref[...], w_ref[...]) + b_ref[...]
    o_ref[...] = x_ref[...] + mod_ref[2:3, :] * dx


def _window_attention(x2, q, kt, v, ktc, vc, sink, mod_l, batch, seq_len, w_bf, b):
    t_rows, d = x2.shape
    kvw = v.shape[1]
    lc = vc.shape[0] // batch
    tq = min(512, seq_len)
    n_qt = seq_len // tq
    wpt = tq // WINDOW
    nwb = t_rows // WINDOW
    tile = lambda bi, i: bi * n_qt + i
    prev_blk = lambda bi, i: jnp.maximum(tile(bi, i) * wpt - 1, 0)
    next_blk = lambda bi, i: jnp.minimum((tile(bi, i) + 1) * wpt, nwb - 1)
    qrow = lambda w: pl.BlockSpec((tq, w), lambda bi, i: (tile(bi, i), 0))
    return pl.pallas_call(
        functools.partial(_attn_kernel, tq=tq, n_qt=n_qt),
        grid=(batch, n_qt),
        in_specs=[pl.BlockSpec(memory_space=pltpu.SMEM),
                  qrow(d), qrow(q.shape[1]),
                  pl.BlockSpec((kvw, WINDOW), lambda bi, i: (0, prev_blk(bi, i))),
                  pl.BlockSpec((kvw, tq), lambda bi, i: (0, tile(bi, i))),
                  pl.BlockSpec((kvw, WINDOW), lambda bi, i: (0, next_blk(bi, i))),
                  pl.BlockSpec((kvw, lc), lambda bi, i: (0, bi)),
                  pl.BlockSpec((WINDOW, kvw), lambda bi, i: (prev_blk(bi, i), 0)),
                  qrow(kvw),
                  pl.BlockSpec((WINDOW, kvw), lambda bi, i: (next_blk(bi, i), 0)),
                  pl.BlockSpec((lc, kvw), lambda bi, i: (bi, 0)),
                  pl.BlockSpec((None, MOD_CHUNKS, d), lambda bi, i: (bi, 0, 0)),
                  _resident((d, d)), pl.BlockSpec((1, d), lambda bi, i: (0, 0))],
        out_specs=qrow(d),
        out_shape=jax.ShapeDtypeStruct((t_rows, d), F32),
        scratch_shapes=[pltpu.VMEM((tq, d), BF16)],
        compiler_params=_params(2),
        name="window_attention",
    )(sink, x2, q, kt, kt, kt, ktc, v, v, v, vc, mod_l, w_bf, b.reshape(1, d))


def _latent_mod_index(t, tiles_per_seq):
    return t // tiles_per_seq


def kernel(x, c, ctx, c_ctx, mod_w, mod_b, norm1_w, norm2_w, mlp_w1, mlp_w2, hy_w_in, hy_b_in, hy_conv_w,
           hy_conv_b, hy_f_w1, hy_f_b1, hy_f_freq1, hy_f_w2, hy_f_b2, hy_f_freq2, hy_f_w3, hy_skip, hy_w_out,
           hy_b_out, at_w_qkv, at_b_qkv, at_q_norm, at_k_norm, at_sink, at_w_out, at_b_out):
    batch, seq_len, d = x.shape
    lc = ctx.shape[1]
    ctx_row = batch

    def ctx_mod_index(t, tiles_per_seq):
        return ctx_row

    cond_rows = -(-(batch + 1) // 8) * 8
    cc = jnp.zeros((cond_rows, d), F32).at[:batch].set(c).at[batch].set(c_ctx)
    mod = _modulation(cc, mod_w, mod_b)

    x2 = x.reshape(batch * seq_len, d)
    c2 = ctx.reshape(batch * lc, d)

    w_in = hy_w_in[0].astype(BF16)
    w_out = hy_w_out[0].astype(BF16)
    w1 = mlp_w1[0].astype(BF16)
    w2 = mlp_w2[0].astype(BF16)
    fpar = (hy_f_w1[0], hy_f_b1[0], hy_f_freq1[0], hy_f_w2[0], hy_f_b2[0], hy_f_freq2[0], hy_f_w3[0])
    def hyena_layer(tok, n_seq, idx_fn, conv):
        x0, u = _hyena_inproj(tok, mod[0], idx_fn, n_seq, norm1_w[0], w_in, hy_b_in[0], hy_conv_w[0],
                              hy_conv_b[0])
        kc, nrm = _hyena_filter(n_seq, d, *fpar)
        yc = conv(u.reshape(batch, n_seq, d), kc, nrm).reshape(batch * n_seq, d)
        return _hyena_out_mlp(tok, x0, u, yc, mod[0], idx_fn, n_seq, hy_skip[0], w_out, hy_b_out[0],
                              norm2_w[0], w1, w2)

    x2 = hyena_layer(x2, seq_len, _latent_mod_index, _long_conv)
    c2 = hyena_layer(c2, lc, ctx_mod_index, _short_seq_conv)

    w_qkv = at_w_qkv[0].astype(BF16)
    q_width = N_HEADS * HEAD_DIM
    rope = _rope_tables(seq_len)
    q, kt, v = _qkv_proj(x2, mod[1], _latent_mod_index, seq_len, norm1_w[1], w_qkv, at_b_qkv[0],
                         at_q_norm[0], at_k_norm[0], True, rope)
    kt_ctx, v_ctx = _qkv_proj(c2, mod[1], ctx_mod_index, lc, norm1_w[1], w_qkv[:, q_width:],
                              at_b_qkv[0][q_width:], at_q_norm[0], at_k_norm[0], False, None)
    x2 = _window_attention(x2, q, kt, v, kt_ctx, v_ctx, at_sink[0].astype(F32) * LOG2_E, mod[1], batch,
                           seq_len, at_w_out[0].astype(BF16), at_b_out[0])
    x2 = _mlp(x2, mod[1], _latent_mod_index, seq_len, norm2_w[1], mlp_w1[1].astype(BF16),
              mlp_w2[1].astype(BF16))
    return x2.reshape(batch, seq_len, d)
```

```python
import functools
import math

import numpy as np
import jax
import jax.numpy as jnp
from jax import lax
from jax.experimental import pallas as pl
from jax.experimental.pallas import tpu as pltpu

F32 = jnp.float32
BF16 = jnp.bfloat16

EPS = 1e-6
MOD_CHUNKS = 6
GRID_W = 64
HY_SHORT = 3
HY_BANDS = 16
HY_DECAY_TARGET = 1e-2
HY_MAX_DECAY = math.log(HY_DECAY_TARGET) / 0.3
HY_MIN_DECAY = math.log(HY_DECAY_TARGET) / 1.5
N_HEADS = 16
N_KV_HEADS = 4
GROUP = N_HEADS // N_KV_HEADS
HEAD_DIM = 64
WINDOW = 128
ROPE_BASE = 10000.0
ROPE_NFREQ = HEAD_DIM // 4
LOG2_E = math.log2(math.e)
Q_SCALE_LOG2 = HEAD_DIM ** -0.5 * LOG2_E

V7X_VMEM_BYTES = 64 * 1024 * 1024
V7X_MXU_DIM = 256
VMEM_LIMIT = V7X_VMEM_BYTES - 8 * 1024 * 1024
LANES = 128
BF16_SUBLANES = 16
FFT_N2 = 128
NEG_BIG = -1e30
ATT_QB = 256


def _params(n_axes, flags=None):
    return pltpu.CompilerParams(dimension_semantics=("arbitrary",) * n_axes,
                                vmem_limit_bytes=VMEM_LIMIT, flags=flags)


def _split_bf16(a):
    hi = a.astype(BF16)
    lo = (a - hi.astype(F32)).astype(BF16)
    return hi, lo


def _dot(a, b):
    return jnp.dot(a, b, preferred_element_type=F32)


def _dot3(a, b):
    ah, al = _split_bf16(a)
    bh, bl = _split_bf16(b)
    return _dot(ah, bh) + (_dot(ah, bl) + _dot(al, bh))


def _norm_mod(x, g, shift, scale):
    ms = jnp.mean(x * x, axis=-1, keepdims=True)
    return x * lax.rsqrt(ms + EPS) * (g * (1.0 + scale)) + shift


def _full(shape):
    nd = len(shape)
    return pl.BlockSpec(shape, lambda *_: (0,) * nd)


def _mod_kernel(c_ref, w_ref, b_ref, o_ref):
    c = c_ref[...]
    s = c * jax.nn.sigmoid(c)
    o_ref[...] = _dot3(s, w_ref[...]) + b_ref[...]


def _modulation(cc, mod_w, mod_b):
    depth, d, n = mod_w.shape
    r = cc.shape[0]
    tn = 1536
    out = pl.pallas_call(
        _mod_kernel,
        grid=(depth, n // tn),
        in_specs=[pl.BlockSpec((r, d), lambda i, j: (0, 0)),
                  pl.BlockSpec((None, d, tn), lambda i, j: (i, 0, j)),
                  pl.BlockSpec((None, 1, tn), lambda i, j: (i, 0, j))],
        out_specs=pl.BlockSpec((None, r, tn), lambda i, j: (i, 0, j)),
        out_shape=jax.ShapeDtypeStruct((depth, r, n), F32),
        compiler_params=_params(2),
        name="modulation",
    )(cc, mod_w, mod_b.reshape(depth, 1, n))
    return out.reshape(depth, r, MOD_CHUNKS, d)


def _inproj_kernel(x_ref, xp_ref, xn_ref, mod_ref, g_ref, w_ref, b_ref, cw_ref, cb_ref,
                   x0_ref, u_ref, p0_ref, p1_ref, p2_ref, *, tm, tiles_per_seq, halo, cw_cols):
    t = pl.program_id(0)
    first = (t % tiles_per_seq) == 0
    last = (t % tiles_per_seq) == tiles_per_seq - 1
    g = g_ref[...]
    shift = mod_ref[0:1, :]
    scale = mod_ref[1:2, :]

    def nm(v):
        return _norm_mod(v, g, shift, scale).astype(BF16)

    h = jnp.concatenate([nm(xp_ref[...]), nm(x_ref[...]), nm(xn_ref[...])], axis=0)
    d = x_ref.shape[1]
    row = lax.broadcasted_iota(jnp.int32, (halo, cw_cols), 0)
    kill_top = jnp.logical_and(first, row == halo - 1)
    kill_bot = jnp.logical_and(last, row == 0)

    def conv(col0, p_ref):
        cols = slice(col0, col0 + cw_cols)
        bias = b_ref[:, cols]
        p = _dot(h, w_ref[:, cols])
        p_ref[0:halo, :] = jnp.where(kill_top, -bias, p[0:halo])
        p_ref[halo:tm + halo, :] = p[halo:tm + halo]
        p_ref[tm + halo:tm + 2 * halo, :] = jnp.where(kill_bot, -bias, p[tm + halo:tm + 2 * halo])
        out = cb_ref[:, cols] + bias * (cw_ref[0:1, cols] + cw_ref[1:2, cols] + cw_ref[2:3, cols])
        for k in range(HY_SHORT):
            out = out + p_ref[halo - 1 + k:halo - 1 + k + tm, :] * cw_ref[k:k + 1, cols]
        return out

    for c0 in range(0, d, cw_cols):
        x0_ref[:, c0:c0 + cw_cols] = conv(c0, p0_ref).astype(BF16)
        u_ref[:, c0:c0 + cw_cols] = (conv(d + c0, p1_ref) * conv(2 * d + c0, p2_ref)).astype(BF16)


def _hyena_inproj(x2, mod_l, mod_index, seq_len, g, w_bf, b, conv_w, conv_b):
    t_rows, d = x2.shape
    tm = min(512, seq_len)
    halo = 16
    tps = seq_len // tm
    nblk = t_rows // halo
    cw_cols = 256
    kern = functools.partial(_inproj_kernel, tm=tm, tiles_per_seq=tps, halo=halo, cw_cols=cw_cols)
    return pl.pallas_call(
        kern,
        grid=(t_rows // tm,),
        in_specs=[pl.BlockSpec((tm, d), lambda t: (t, 0)),
                  pl.BlockSpec((halo, d), lambda t: (jnp.maximum(t * (tm // halo) - 1, 0), 0)),
                  pl.BlockSpec((halo, d), lambda t: (jnp.minimum((t + 1) * (tm // halo), nblk - 1), 0)),
                  pl.BlockSpec((None, MOD_CHUNKS, d), lambda t: (mod_index(t, tps), 0, 0)),
                  _full((1, d)), _full((d, 3 * d)), _full((1, 3 * d)),
                  _full((HY_SHORT, 3 * d)), _full((1, 3 * d))],
        out_specs=[pl.BlockSpec((tm, d), lambda t: (t, 0)),
                   pl.BlockSpec((tm, d), lambda t: (t, 0))],
        out_shape=[jax.ShapeDtypeStruct((t_rows, d), BF16),
                   jax.ShapeDtypeStruct((t_rows, d), BF16)],
        scratch_shapes=[pltpu.VMEM((tm + 2 * halo, cw_cols), F32)] * 3,
        compiler_params=_params(1),
        name="hyena_inproj",
    )(x2, x2, x2, mod_l, g.reshape(1, d), w_bf, b.reshape(1, 3 * d), conv_w, conv_b.reshape(1, 3 * d))


def _filter_hidden_kernel(zt_ref, w1t_ref, b1_ref, f1_ref, w2t_ref, b2_ref, f2_ref, o_ref):
    h = jnp.sin(f1_ref[...] * (_dot3(w1t_ref[...], zt_ref[...]) + b1_ref[...]))
    o_ref[...] = jnp.sin(f2_ref[...] * (_dot3(w2t_ref[...], h) + b2_ref[...]))


def _filter_out_kernel(h_ref, w3_ref, dl_ref, o_ref, sum_ref, *, seq_len):
    i = pl.program_id(1)
    tr, td = o_ref.shape
    r = lax.broadcasted_iota(jnp.int32, (tr, td), 0) + i * tr
    pos = jnp.where(r < seq_len, r, 2 * seq_len - r)
    t = pos.astype(F32) / seq_len
    kc = _dot3(h_ref[...], w3_ref[...]) * jnp.exp(-t * dl_ref[...])
    kc = jnp.where(r == seq_len, 0.0, kc)
    o_ref[...] = kc

    @pl.when(i == 0)
    def _():
        sum_ref[...] = jnp.zeros_like(sum_ref)

    sum_ref[...] += jnp.sum(jnp.abs(kc), axis=0, keepdims=True)


def _filter_features(seq_len):
    f32 = jnp.float32
    pos_f = jnp.arange(seq_len, dtype=f32)
    pos_b = ((seq_len - jnp.arange(seq_len)) % seq_len).astype(f32)
    pos = jnp.concatenate([pos_f, pos_b])
    t = pos / seq_len
    bands = jnp.linspace(1e-4, HY_BANDS - 1, HY_BANDS, dtype=f32)
    ang = (2.0 * math.pi / seq_len) * pos[:, None] * bands[None, :]
    return jnp.concatenate([t[:, None], jnp.cos(ang), -jnp.sin(ang)], axis=-1)


def _hyena_filter(seq_len, d, f_w1, f_b1, f_fr1, f_w2, f_b2, f_fr2, f_w3):
    z = _filter_features(seq_len)
    emb = z.shape[1]
    hid = f_w1.shape[1]
    rows = 2 * seq_len
    tr = min(2048, seq_len)
    h2t = pl.pallas_call(
        _filter_hidden_kernel,
        grid=(rows // tr,),
        in_specs=[pl.BlockSpec((emb, tr), lambda i: (0, i)),
                  _full((hid, emb)), _full((hid, 1)), _full((hid, 1)),
                  _full((hid, hid)), _full((hid, 1)), _full((hid, 1))],
        out_specs=pl.BlockSpec((hid, tr), lambda i: (0, i)),
        out_shape=jax.ShapeDtypeStruct((hid, rows), F32),
        compiler_params=_params(1),
        name="hyena_filter_hidden",
    )(z.T, f_w1.T, f_b1.reshape(hid, 1), f_fr1.reshape(hid, 1), f_w2.T, f_b2.reshape(hid, 1),
      f_fr2.reshape(hid, 1))
    h2 = h2t.T
    deltas = jnp.abs(jnp.linspace(HY_MIN_DECAY, HY_MAX_DECAY, d, dtype=F32)).reshape(1, d)
    td = 256
    nd = d // td
    tiles_per_half = seq_len // tr
    return pl.pallas_call(
        functools.partial(_filter_out_kernel, seq_len=seq_len),
        grid=(nd, rows // tr),
        in_specs=[pl.BlockSpec((tr, hid), lambda j, i: (i, 0)),
                  pl.BlockSpec((hid, td), lambda j, i: (0, (i // tiles_per_half) * nd + j)),
                  pl.BlockSpec((1, td), lambda j, i: (0, j))],
        out_specs=[pl.BlockSpec((tr, td), lambda j, i: (i, j)),
                   pl.BlockSpec((1, td), lambda j, i: (0, j))],
        out_shape=[jax.ShapeDtypeStruct((rows, d), F32), jax.ShapeDtypeStruct((1, d), F32)],
        compiler_params=_params(2),
        name="hyena_filter_out",
    )(h2, f_w3, deltas)


def _dft_tables(n1):
    n2 = FFT_N2
    n = n1 * n2
    h1 = n1 // 2
    k1 = np.concatenate([np.arange(0, h1 + 1), np.arange(1, h1)])
    is_im = np.concatenate([np.zeros(h1 + 1, bool), np.ones(h1 - 1, bool)])
    th = 2.0 * np.pi * np.outer(k1, np.arange(n1)) / n1
    f1 = np.where(is_im[:, None], -np.sin(th), np.cos(th))
    wgt = np.where((k1 == 0) | (k1 == h1), 1.0, 2.0)[:, None] / n
    f1inv = (np.where(is_im[:, None], -np.sin(th), np.cos(th)) * wgt).T
    m = np.arange(n2)
    k2 = np.arange(n2)

    def g(k1v):
        ph = 2.0 * np.pi * np.outer(k2 * n1 + k1v, m) / n
        return np.cos(ph), -np.sin(ph)

    mf = np.zeros((h1, 2 * n2, 2 * n2))
    for j in range(1, h1):
        gr, gi = g(j)
        mf[j] = np.block([[gr, -gi], [gi, gr]])
    g0r, g0i = g(0)
    ghr, ghi = g(h1)
    zz = np.zeros((n2, n2))
    m0 = np.block([[g0r, zz], [g0i, zz], [zz, ghr], [zz, ghi]])
    mfi = np.transpose(mf, (0, 2, 1))
    m0i = m0.T
    as32 = lambda a: jnp.asarray(a.astype(np.float32))
    return dict(f1=as32(f1), f1inv=as32(f1inv), mf=as32(mf), mfi=as32(mfi), m0=as32(m0), m0i=as32(m0i))


def _stage1_kernel(f_ref, x_ref, o_ref):
    xt = jnp.swapaxes(x_ref[...].astype(BF16), 0, 1)
    f = f_ref[...]
    out = jnp.stack([_dot(f, xt[t]).astype(o_ref.dtype) for t in range(xt.shape[0])], axis=0)
    o_ref[...] = jnp.swapaxes(out, 0, 1)


def _dft_stage1(f_bf, x4, out_dtype):
    bx, r, n2, d = x4.shape
    m = f_bf.shape[0]
    tt = (2 if x4.dtype == BF16 else 1) * BF16_SUBLANES
    return pl.pallas_call(
        _stage1_kernel,
        grid=(bx, n2 // tt),
        in_specs=[_full((m, r)), pl.BlockSpec((None, r, tt, d), lambda b, j: (b, 0, j, 0))],
        out_specs=pl.BlockSpec((None, m, tt, d), lambda b, j: (b, 0, j, 0)),
        out_shape=jax.ShapeDtypeStruct((bx, m, n2, d), out_dtype),
        compiler_params=_params(2),
        name="dft_stage1",
    )(f_bf, x4)


def _cmul(xr, xi, kr, ki):
    return xr * kr - xi * ki, xr * ki + xi * kr


def _stage2_kernel(a_ref, ak_ref, nrm_ref, mf_ref, mfi_ref, m0_ref, m0i_ref, o_ref):
    j = pl.program_id(0)
    n2 = FFT_N2
    nb = a_ref.shape[0]
    d = a_ref.shape[-1]
    ak = ak_ref[...].reshape(2 * n2, d)
    nrm = nrm_ref[...]

    @pl.when(j == 0)
    def _():
        k = _dot(m0_ref[...], ak) / nrm
        for b in range(nb):
            a = a_ref[b].reshape(2 * n2, d)
            x = _dot(m0_ref[...], a)
            y0r, y0i = _cmul(x[0:n2], x[n2:2 * n2], k[0:n2], k[n2:2 * n2])
            y1r, y1i = _cmul(x[2 * n2:3 * n2], x[3 * n2:4 * n2], k[2 * n2:3 * n2], k[3 * n2:4 * n2])
            y = jnp.concatenate([y0r, y0i, y1r, y1i], axis=0).astype(BF16)
            o_ref[b] = _dot(m0i_ref[...], y).reshape(2, n2, d).astype(o_ref.dtype)

    @pl.when(j > 0)
    def _():
        k = _dot(mf_ref[...], ak) / nrm
        for b in range(nb):
            a = a_ref[b].reshape(2 * n2, d)
            x = _dot(mf_ref[...], a)
            yr, yi = _cmul(x[0:n2], x[n2:2 * n2], k[0:n2], k[n2:2 * n2])
            y = jnp.concatenate([yr, yi], axis=0).astype(BF16)
            o_ref[b] = _dot(mfi_ref[...], y).reshape(2, n2, d).astype(o_ref.dtype)


def _dft_stage2(a5, ak, nrm, tabs):
    nb, _, h1, n2, d = a5.shape
    blk = pl.BlockSpec((nb, 2, None, n2, d), lambda j: (0, 0, j, 0, 0))
    return pl.pallas_call(
        _stage2_kernel,
        grid=(h1,),
        in_specs=[blk,
                  pl.BlockSpec((2, None, n2, d), lambda j: (0, j, 0, 0)),
                  _full((1, d)),
                  pl.BlockSpec((None, 2 * n2, 2 * n2), lambda j: (j, 0, 0)),
                  pl.BlockSpec((None, 2 * n2, 2 * n2), lambda j: (j, 0, 0)),
                  _full((4 * n2, 2 * n2)), _full((2 * n2, 4 * n2))],
        out_specs=blk,
        out_shape=jax.ShapeDtypeStruct(a5.shape, BF16),
        compiler_params=_params(1),
        name="dft_stage2",
    )(a5, ak, nrm, tabs["mf"].astype(BF16), tabs["mfi"].astype(BF16), tabs["m0"].astype(BF16),
      tabs["m0i"].astype(BF16))


def _long_conv(u, kc, nrm):
    b, seq_len, d = u.shape
    n2 = FFT_N2
    n1 = 2 * seq_len // n2
    h1 = n1 // 2
    tabs = _dft_tables(n1)
    f1 = tabs["f1"].astype(BF16)
    ak = _dft_stage1(f1, kc.reshape(1, n1, n2, d), BF16)
    a = _dft_stage1(f1[:, :h1], u.reshape(b, h1, n2, d), BF16)
    bv = _dft_stage2(a.reshape(b, 2, h1, n2, d), ak.reshape(2, h1, n2, d), nrm, tabs)
    y = _dft_stage1(tabs["f1inv"][:h1].astype(BF16), bv.reshape(b, n1, n2, d), BF16)
    return y.reshape(b, seq_len, d)


def _ctxconv_kernel(u_ref, kc_ref, nrm_ref, cf_ref, ci_ref, o_ref, *, seq_len):
    nf = 2 * seq_len
    kh, kl = _split_bf16(kc_ref[...])
    ksp = (_dot(cf_ref[...], kh) + _dot(cf_ref[...], kl)) / nrm_ref[...]
    kr, ki = ksp[0:nf], ksp[nf:2 * nf]
    for b in range(u_ref.shape[0]):
        x = _dot(cf_ref[:, 0:seq_len], u_ref[b])
        yr, yi = _cmul(x[0:nf], x[nf:2 * nf], kr, ki)
        y = jnp.concatenate([yr, yi], axis=0).astype(BF16)
        o_ref[b] = _dot(ci_ref[...], y).astype(o_ref.dtype)


def _short_seq_conv(u, kc, nrm):
    b, seq_len, d = u.shape
    nf = 2 * seq_len
    ph = 2.0 * np.pi * np.outer(np.arange(nf), np.arange(nf)) / nf
    cf = jnp.asarray(np.concatenate([np.cos(ph), -np.sin(ph)], axis=0).astype(np.float32))
    ci = jnp.asarray((np.concatenate([np.cos(ph), -np.sin(ph)], axis=1)[:seq_len] / nf).astype(np.float32))
    return pl.pallas_call(
        functools.partial(_ctxconv_kernel, seq_len=seq_len),
        grid=(1,),
        in_specs=[_full((b, seq_len, d)), _full((nf, d)), _full((1, d)), _full((2 * nf, nf)),
                  _full((seq_len, 2 * nf))],
        out_specs=_full((b, seq_len, d)),
        out_shape=jax.ShapeDtypeStruct((b, seq_len, d), BF16),
        compiler_params=_params(1),
        name="hyena_ctx_conv",
    )(u, kc, nrm, cf.astype(BF16), ci.astype(BF16))


def _mlp_body(x, mod_ref, g_ref, w1_ref, w2_ref, ff_chunk):
    h = _norm_mod(x, g_ref[...], mod_ref[3:4, :], mod_ref[4:5, :]).astype(BF16)
    acc = jnp.zeros(x.shape, F32)
    for c0 in range(0, w1_ref.shape[1], ff_chunk):
        a = jnp.maximum(_dot(h, w1_ref[:, c0:c0 + ff_chunk]), 0.0)
        acc = acc + _dot((a * a).astype(BF16), w2_ref[c0:c0 + ff_chunk, :])
    return x + mod_ref[5:6, :] * acc


def _hy_out_mlp_kernel(x_ref, x0_ref, u_ref, yc_ref, mod_ref, skip_ref, w_ref, b_ref, g_ref, w1_ref, w2_ref,
                       o_ref, *, ff_chunk):
    u = u_ref[...].astype(F32)
    y = x0_ref[...].astype(F32) * (yc_ref[...].astype(F32) + u * skip_ref[...])
    dx = _dot(y.astype(BF16), w_ref[...]) + b_ref[...]
    x = x_ref[...] + mod_ref[2:3, :] * dx
    o_ref[...] = _mlp_body(x, mod_ref, g_ref, w1_ref, w2_ref, ff_chunk)


def _resident(shape):
    nd = len(shape)
    return pl.BlockSpec(shape, lambda *_: (0,) * nd, pipeline_mode=pl.Buffered(1))


MLP_FF_CHUNK = 1024


def _hyena_out_mlp(x2, x0, u, yc, mod_l, mod_index, seq_len, skip, w_bf, b, g, w1_bf, w2_bf):
    t_rows, d = x2.shape
    ff = w1_bf.shape[1]
    tm = min(512, seq_len)
    tps = seq_len // tm
    row = pl.BlockSpec((tm, d), lambda t: (t, 0))
    return pl.pallas_call(
        functools.partial(_hy_out_mlp_kernel, ff_chunk=MLP_FF_CHUNK),
        grid=(t_rows // tm,),
        in_specs=[row, row, row, row,
                  pl.BlockSpec((None, MOD_CHUNKS, d), lambda t: (mod_index(t, tps), 0, 0)),
                  _full((1, d)), _resident((d, d)), _full((1, d)),
                  _full((1, d)), _resident((d, ff)), _resident((ff, d))],
        out_specs=row,
        out_shape=jax.ShapeDtypeStruct((t_rows, d), F32),
        compiler_params=_params(1),
        name="hyena_out_mlp",
    )(x2, x0, u, yc, mod_l, skip.reshape(1, d), w_bf, b.reshape(1, d), g.reshape(1, d), w1_bf, w2_bf)


def _mlp_kernel(x_ref, mod_ref, g_ref, w1_ref, w2_ref, o_ref, *, ff_chunk):
    o_ref[...] = _mlp_body(x_ref[...], mod_ref, g_ref, w1_ref, w2_ref, ff_chunk)


def _mlp(x2, mod_l, mod_index, seq_len, g, w1_bf, w2_bf):
    t_rows, d = x2.shape
    ff = w1_bf.shape[1]
    tm = min(512, seq_len)
    tps = seq_len // tm
    row = pl.BlockSpec((tm, d), lambda t: (t, 0))
    return pl.pallas_call(
        functools.partial(_mlp_kernel, ff_chunk=MLP_FF_CHUNK),
        grid=(t_rows // tm,),
        in_specs=[row,
                  pl.BlockSpec((None, MOD_CHUNKS, d), lambda t: (mod_index(t, tps), 0, 0)),
                  _full((1, d)), _resident((d, ff)), _resident((ff, d))],
        out_specs=row,
        out_shape=jax.ShapeDtypeStruct((t_rows, d), F32),
        compiler_params=_params(1),
        name="mlp",
    )(x2, mod_l, g.reshape(1, d), w1_bf, w2_bf)


def _rope(v, cos, sin_signed, hi_half):
    partner = jnp.where(hi_half, pltpu.roll(v, 16, axis=1), pltpu.roll(v, LANES - 16, axis=1))
    return v * cos + partner * sin_signed


def _qkv_kernel(*refs, with_q, with_rope, q_width, kv_width):
    x_ref, mod_ref, g_ref, w_ref, b_ref, bd_ref, qn_ref, kn_ref = refs[:8]
    idx = 8
    if with_rope:
        cos_ref, sin_ref = refs[idx:idx + 2]
        idx += 2
    outs = refs[idx:]
    if with_q:
        q_ref, kt_ref, v_ref = outs
    else:
        kt_ref, v_ref = outs
    h = _norm_mod(x_ref[...], g_ref[...], mod_ref[0:1, :], mod_ref[1:2, :]).astype(BF16)
    tm = h.shape[0]
    if with_rope:
        cos = cos_ref[...]
        sin = sin_ref[...]
        hi_half = (lax.broadcasted_iota(jnp.int32, (tm, LANES), 1) & 16) != 0

    def head_norm(p, w, bd):
        sq = (p * p).astype(BF16)
        sw = bd.shape[0]
        ms = jnp.concatenate([_dot(sq[:, c0:c0 + sw], bd) for c0 in range(0, p.shape[1], sw)], axis=1)
        return p * lax.rsqrt(ms + EPS) * w

    def rotate(p):
        if not with_rope:
            return p
        return jnp.concatenate([_rope(p[:, c0:c0 + LANES], cos, sin, hi_half)
                                for c0 in range(0, p.shape[1], LANES)], axis=1)

    off = 0
    if with_q:
        q = _dot(h, w_ref[:, 0:q_width]) + b_ref[:, 0:q_width]
        q_ref[...] = rotate(head_norm(q, qn_ref[...], bd_ref[...]) * Q_SCALE_LOG2).astype(q_ref.dtype)
        off = q_width
    k = _dot(h, w_ref[:, off:off + kv_width]) + b_ref[:, off:off + kv_width]
    k = rotate(head_norm(k, kn_ref[...], bd_ref[...]))
    kt_ref[...] = k.T.astype(kt_ref.dtype)
    v = _dot(h, w_ref[:, off + kv_width:off + 2 * kv_width]) + b_ref[:, off + kv_width:off + 2 * kv_width]
    v_ref[...] = v.astype(v_ref.dtype)


def _rope_tables(seq_len):
    rows = seq_len // GRID_W
    row = jnp.repeat(jnp.arange(rows, dtype=F32), GRID_W)
    col = jnp.tile(jnp.arange(GRID_W, dtype=F32), rows)
    inv = ROPE_BASE ** (-jnp.arange(ROPE_NFREQ, dtype=F32) / ROPE_NFREQ)
    ar = row[:, None] * inv[None, :]
    ac = col[:, None] * inv[None, :]
    cos = jnp.concatenate([jnp.cos(ar), jnp.cos(ar), jnp.cos(ac), jnp.cos(ac)], axis=-1)
    sin = jnp.concatenate([-jnp.sin(ar), jnp.sin(ar), -jnp.sin(ac), jnp.sin(ac)], axis=-1)
    return jnp.tile(cos, (1, LANES // HEAD_DIM)), jnp.tile(sin, (1, LANES // HEAD_DIM))


def _qkv_proj(x2, mod_l, mod_index, seq_len, g, w_bf, b, qn, kn, with_q, rope):
    t_rows, d = x2.shape
    q_width = N_HEADS * HEAD_DIM
    kv_width = N_KV_HEADS * HEAD_DIM
    width = w_bf.shape[1]
    tm = min(512, seq_len)
    tps = seq_len // tm
    slab_heads = V7X_MXU_DIM // HEAD_DIM
    bd = jnp.asarray(np.kron(np.eye(slab_heads), np.full((HEAD_DIM, HEAD_DIM), 1.0 / HEAD_DIM)).astype(np.float32))
    row = lambda w: pl.BlockSpec((tm, w), lambda t: (t, 0))
    in_specs = [row(d),
                pl.BlockSpec((None, MOD_CHUNKS, d), lambda t: (mod_index(t, tps), 0, 0)),
                _full((1, d)), _full((d, width)), _full((1, width)), _full((V7X_MXU_DIM, V7X_MXU_DIM)),
                _full((1, q_width)), _full((1, kv_width))]
    args = [x2, mod_l, g.reshape(1, d), w_bf, b.reshape(1, width), bd.astype(BF16),
            jnp.tile(qn, N_HEADS).reshape(1, q_width), jnp.tile(kn, N_KV_HEADS).reshape(1, kv_width)]
    if rope is not None:
        in_specs += [pl.BlockSpec((tm, LANES), lambda t: (t % tps, 0))] * 2
        args += list(rope)
    out_specs = [pl.BlockSpec((kv_width, tm), lambda t: (0, t)), row(kv_width)]
    out_shape = [jax.ShapeDtypeStruct((kv_width, t_rows), BF16), jax.ShapeDtypeStruct((t_rows, kv_width), BF16)]
    if with_q:
        out_specs = [row(q_width)] + out_specs
        out_shape = [jax.ShapeDtypeStruct((t_rows, q_width), BF16)] + out_shape
    return pl.pallas_call(
        functools.partial(_qkv_kernel, with_q=with_q, with_rope=rope is not None,
                          q_width=q_width, kv_width=kv_width),
        grid=(t_rows // tm,),
        in_specs=in_specs, out_specs=out_specs, out_shape=out_shape,
        compiler_params=_params(1),
        name="qkv_proj",
    )(*args)


def _attn_kernel(sink_ref, x_ref, q_ref, ktp_ref, ktm_ref, ktn_ref, ktc_ref, vp_ref, vm_ref, vn_ref, vc_ref,
                 mod_ref, w_ref, b_ref, o_ref, att_ref, *, tq, n_qt):
    i = pl.program_id(1)
    lc = vc_ref.shape[0]
    qb = min(ATT_QB, tq)
    n_sub = qb // WINDOW
    nw = qb + 2 * WINDOW
    nk = nw + lc
    sw = 3 * WINDOW
    sk = sw + lc
    n_rb = tq // qb
    r = lax.broadcasted_iota(jnp.int32, (WINDOW, WINDOW), 0)
    c = lax.broadcasted_iota(jnp.int32, (WINDOW, WINDOW), 1)
    before_bias = jnp.where(c >= r, 0.0, NEG_BIG)
    after_bias = jnp.where(c <= r, 0.0, NEG_BIG)
    first_before = jnp.where(jnp.logical_and(c >= r, i > 0), 0.0, NEG_BIG)
    last_after = jnp.where(jnp.logical_and(c <= r, i < n_qt - 1), 0.0, NEG_BIG)

    def masked_scores(raw, rb, j):
        rows = slice(j * WINDOW, (j + 1) * WINDOW)
        lo = j * WINDOW
        bb = first_before if (rb == 0 and j == 0) else before_bias
        ba = last_after if (rb == n_rb - 1 and j == n_sub - 1) else after_bias
        return jnp.concatenate([raw[rows, lo:lo + WINDOW] + bb,
                                raw[rows, lo + WINDOW:lo + 2 * WINDOW],
                                raw[rows, lo + 2 * WINDOW:lo + sw] + ba,
                                raw[rows, nw:nk]], axis=1)

    zeros = jnp.zeros((HEAD_DIM, nk), BF16)
    for hk in range(N_KV_HEADS):
        hs = slice(hk * HEAD_DIM, (hk + 1) * HEAD_DIM)
        ktw = jnp.concatenate([ktp_ref[hs, :], ktm_ref[hs, :], ktn_ref[hs, :]], axis=1)
        vw = jnp.concatenate([vp_ref[:, hs], vm_ref[:, hs], vn_ref[:, hs]], axis=0)
        ktc = ktc_ref[hs, :]
        vcx = vc_ref[:, hs]
        heads = [hk * GROUP + gq for gq in range(GROUP)]
        for rb in range(n_rb):
            rows = slice(rb * qb, (rb + 1) * qb)
            kk = jnp.concatenate([ktw[:, rb * qb:rb * qb + nw], ktc], axis=1)
            vv = jnp.concatenate([vw[rb * qb:rb * qb + nw], vcx], axis=0)
            kk_pair = (jnp.concatenate([kk, zeros], axis=0), jnp.concatenate([zeros, kk], axis=0))
            raw = [_dot(q_ref[rows, (hd // 2) * LANES:(hd // 2 + 1) * LANES], kk_pair[hd % 2]) for hd in heads]
            chains = [(gq, j) for gq in range(GROUP) for j in range(n_sub)]
            ss = [masked_scores(raw[gq], rb, j) for gq, j in chains]
            ms = [jnp.maximum(jnp.max(s, axis=-1, keepdims=True), sink_ref[heads[gq]])
                  for s, (gq, j) in zip(ss, chains)]
            ps = [jnp.exp2(s - m) for s, m in zip(ss, ms)]
            dens = [jnp.sum(p, axis=-1, keepdims=True) + jnp.exp2(sink_ref[heads[gq]] - m)
                    for p, m, (gq, j) in zip(ps, ms, chains)]
            for gq, hd in enumerate(heads):
                full = []
                for j in range(n_sub):
                    p = ps[gq * n_sub + j].astype(BF16)
                    parts = [p[:, 0:sw], p[:, sw:sk]]
                    if j > 0:
                        parts.insert(0, jnp.zeros((WINDOW, j * WINDOW), BF16))
                    if nw - sw - j * WINDOW > 0:
                        parts.insert(-1, jnp.zeros((WINDOW, nw - sw - j * WINDOW), BF16))
                    full.append(jnp.concatenate(parts, axis=1))
                den = jnp.concatenate(dens[gq * n_sub:(gq + 1) * n_sub], axis=0)
                o = _dot(jnp.concatenate(full, axis=0), vv) / den
                att_ref[rows, hd * HEAD_DIM:(hd + 1) * HEAD_DIM] = o.astype(BF16)
    dx = _dot(att_ref[...], w_ref[...]) + b_ref[...]
    o_ref[...] = x_ref[...] + mod_ref[2:3, :] * dx


def _window_attention(x2, q, kt, v, ktc, vc, sink, mod_l, batch, seq_len, w_bf, b):
    t_rows, d = x2.shape
    kvw = v.shape[1]
    lc = vc.shape[0] // batch
    tq = min(512, seq_len)
    n_qt = seq_len // tq
    wpt = tq // WINDOW
    nwb = t_rows // WINDOW
    tile = lambda bi, i: bi * n_qt + i
    prev_blk = lambda bi, i: jnp.maximum(tile(bi, i) * wpt - 1, 0)
    next_blk = lambda bi, i: jnp.minimum((tile(bi, i) + 1) * wpt, nwb - 1)
    qrow = lambda w: pl.BlockSpec((tq, w), lambda bi, i: (tile(bi, i), 0))
    return pl.pallas_call(
        functools.partial(_attn_kernel, tq=tq, n_qt=n_qt),
        grid=(batch, n_qt),
        in_specs=[pl.BlockSpec(memory_space=pltpu.SMEM),
                  qrow(d), qrow(q.shape[1]),
                  pl.BlockSpec((kvw, WINDOW), lambda bi, i: (0, prev_blk(bi, i))),
                  pl.BlockSpec((kvw, tq), lambda bi, i: (0, tile(bi, i))),
                  pl.BlockSpec((kvw, WINDOW), lambda bi, i: (0, next_blk(bi, i))),
                  pl.BlockSpec((kvw, lc), lambda bi, i: (0, bi)),
                  pl.BlockSpec((WINDOW, kvw), lambda bi, i: (prev_blk(bi, i), 0)),
                  qrow(kvw),
                  pl.BlockSpec((WINDOW, kvw), lambda bi, i: (next_blk(bi, i), 0)),
                  pl.BlockSpec((lc, kvw), lambda bi, i: (bi, 0)),
                  pl.BlockSpec((None, MOD_CHUNKS, d), lambda bi, i: (bi, 0, 0)),
                  _resident((d, d)), pl.BlockSpec((1, d), lambda bi, i: (0, 0))],
        out_specs=qrow(d),
        out_shape=jax.ShapeDtypeStruct((t_rows, d), F32),
        scratch_shapes=[pltpu.VMEM((tq, d), BF16)],
        compiler_params=_params(2),
        name="window_attention",
    )(sink, x2, q, kt, kt, kt, ktc, v, v, v, vc, mod_l, w_bf, b.reshape(1, d))


def _latent_mod_index(t, tiles_per_seq):
    return t // tiles_per_seq


def kernel(x, c, ctx, c_ctx, mod_w, mod_b, norm1_w, norm2_w, mlp_w1, mlp_w2, hy_w_in, hy_b_in, hy_conv_w,
           hy_conv_b, hy_f_w1, hy_f_b1, hy_f_freq1, hy_f_w2, hy_f_b2, hy_f_freq2, hy_f_w3, hy_skip, hy_w_out,
           hy_b_out, at_w_qkv, at_b_qkv, at_q_norm, at_k_norm, at_sink, at_w_out, at_b_out):
    batch, seq_len, d = x.shape
    lc = ctx.shape[1]
    ctx_row = batch

    def ctx_mod_index(t, tiles_per_seq):
        return ctx_row

    cond_rows = -(-(batch + 1) // 8) * 8
    cc = jnp.zeros((cond_rows, d), F32).at[:batch].set(c).at[batch].set(c_ctx)
    mod = _modulation(cc, mod_w, mod_b)

    x2 = x.reshape(batch * seq_len, d)
    c2 = ctx.reshape(batch * lc, d)

    w_in = hy_w_in[0].astype(BF16)
    w_out = hy_w_out[0].astype(BF16)
    w1 = mlp_w1[0].astype(BF16)
    w2 = mlp_w2[0].astype(BF16)
    fpar = (hy_f_w1[0], hy_f_b1[0], hy_f_freq1[0], hy_f_w2[0], hy_f_b2[0], hy_f_freq2[0], hy_f_w3[0])
    def hyena_layer(tok, n_seq, idx_fn, conv):
        x0, u = _hyena_inproj(tok, mod[0], idx_fn, n_seq, norm1_w[0], w_in, hy_b_in[0], hy_conv_w[0],
                              hy_conv_b[0])
        kc, nrm = _hyena_filter(n_seq, d, *fpar)
        yc = conv(u.reshape(batch, n_seq, d), kc, nrm).reshape(batch * n_seq, d)
        return _hyena_out_mlp(tok, x0, u, yc, mod[0], idx_fn, n_seq, hy_skip[0], w_out, hy_b_out[0],
                              norm2_w[0], w1, w2)

    x2 = hyena_layer(x2, seq_len, _latent_mod_index, _long_conv)
    c2 = hyena_layer(c2, lc, ctx_mod_index, _short_seq_conv)

    w_qkv = at_w_qkv[0].astype(BF16)
    q_width = N_HEADS * HEAD_DIM
    rope = _rope_tables(seq_len)
    q, kt, v = _qkv_proj(x2, mod[1], _latent_mod_index, seq_len, norm1_w[1], w_qkv, at_b_qkv[0],
                         at_q_norm[0], at_k_norm[0], True, rope)
    kt_ctx, v_ctx = _qkv_proj(c2, mod[1], ctx_mod_index, lc, norm1_w[1], w_qkv[:, q_width:],
                              at_b_qkv[0][q_width:], at_q_norm[0], at_k_norm[0], False, None)
    x2 = _window_attention(x2, q, kt, v, kt_ctx, v_ctx, at_sink[0].astype(F32) * LOG2_E, mod[1], batch,
                           seq_len, at_w_out[0].astype(BF16), at_b_out[0])
    x2 = _mlp(x2, mod[1], _latent_mod_index, seq_len, norm2_w[1], mlp_w1[1].astype(BF16),
              mlp_w2[1].astype(BF16))
    return x2.reshape(batch, seq_len, d)
```

```python
import functools
import math

import numpy as np
import jax
import jax.numpy as jnp
from jax import lax
from jax.experimental import pallas as pl
from jax.experimental.pallas import tpu as pltpu

F32 = jnp.float32
BF16 = jnp.bfloat16

EPS = 1e-6
MOD_CHUNKS = 6
GRID_W = 64
HY_SHORT = 3
HY_BANDS = 16
HY_DECAY_TARGET = 1e-2
HY_MAX_DECAY = math.log(HY_DECAY_TARGET) / 0.3
HY_MIN_DECAY = math.log(HY_DECAY_TARGET) / 1.5
N_HEADS = 16
N_KV_HEADS = 4
GROUP = N_HEADS // N_KV_HEADS
HEAD_DIM = 64
WINDOW = 128
ROPE_BASE = 10000.0
ROPE_NFREQ = HEAD_DIM // 4
LOG2_E = math.log2(math.e)
Q_SCALE_LOG2 = HEAD_DIM ** -0.5 * LOG2_E

V7X_VMEM_BYTES = 64 * 1024 * 1024
V7X_MXU_DIM = 256
VMEM_LIMIT = V7X_VMEM_BYTES - 8 * 1024 * 1024
LANES = 128
BF16_SUBLANES = 16
FFT_N2 = 128
NEG_BIG = -1e30
ATT_QB = 256


def _params(n_axes, flags=None):
    return pltpu.CompilerParams(dimension_semantics=("arbitrary",) * n_axes,
                                vmem_limit_bytes=VMEM_LIMIT, flags=flags)


def _split_bf16(a):
    hi = a.astype(BF16)
    lo = (a - hi.astype(F32)).astype(BF16)
    return hi, lo


def _dot(a, b):
    return jnp.dot(a, b, preferred_element_type=F32)


def _dot3(a, b):
    ah, al = _split_bf16(a)
    bh, bl = _split_bf16(b)
    return _dot(ah, bh) + (_dot(ah, bl) + _dot(al, bh))


def _norm_mod(x, g, shift, scale):
    ms = jnp.mean(x * x, axis=-1, keepdims=True)
    return x * lax.rsqrt(ms + EPS) * (g * (1.0 + scale)) + shift


def _full(shape):
    nd = len(shape)
    return pl.BlockSpec(shape, lambda *_: (0,) * nd)


def _mod_kernel(c_ref, w_ref, b_ref, o_ref):
    c = c_ref[...]
    s = c * jax.nn.sigmoid(c)
    o_ref[...] = _dot3(s, w_ref[...]) + b_ref[...]


def _modulation(cc, mod_w, mod_b):
    depth, d, n = mod_w.shape
    r = cc.shape[0]
    tn = 1536
    out = pl.pallas_call(
        _mod_kernel,
        grid=(depth, n // tn),
        in_specs=[pl.BlockSpec((r, d), lambda i, j: (0, 0)),
                  pl.BlockSpec((None, d, tn), lambda i, j: (i, 0, j)),
                  pl.BlockSpec((None, 1, tn), lambda i, j: (i, 0, j))],
        out_specs=pl.BlockSpec((None, r, tn), lambda i, j: (i, 0, j)),
        out_shape=jax.ShapeDtypeStruct((depth, r, n), F32),
        compiler_params=_params(2),
        name="modulation",
    )(cc, mod_w, mod_b.reshape(depth, 1, n))
    return out.reshape(depth, r, MOD_CHUNKS, d)


def _inproj_kernel(x_ref, xp_ref, xn_ref, mod_ref, g_ref, w_ref, b_ref, cw_ref, cb_ref,
                   x0_ref, u_ref, p0_ref, p1_ref, p2_ref, *, tm, tiles_per_seq, halo, cw_cols):
    t = pl.program_id(0)
    first = (t % tiles_per_seq) == 0
    last = (t % tiles_per_seq) == tiles_per_seq - 1
    g = g_ref[...]
    shift = mod_ref[0:1, :]
    scale = mod_ref[1:2, :]

    def nm(v):
        return _norm_mod(v, g, shift, scale).astype(BF16)

    h = jnp.concatenate([nm(xp_ref[...]), nm(x_ref[...]), nm(xn_ref[...])], axis=0)
    d = x_ref.shape[1]
    row = lax.broadcasted_iota(jnp.int32, (halo, cw_cols), 0)
    kill_top = jnp.logical_and(first, row == halo - 1)
    kill_bot = jnp.logical_and(last, row == 0)

    def conv(col0, p_ref):
        cols = slice(col0, col0 + cw_cols)
        bias = b_ref[:, cols]
        p = _dot(h, w_ref[:, cols])
        p_ref[0:halo, :] = jnp.where(kill_top, -bias, p[0:halo])
        p_ref[halo:tm + halo, :] = p[halo:tm + halo]
        p_ref[tm + halo:tm + 2 * halo, :] = jnp.where(kill_bot, -bias, p[tm + halo:tm + 2 * halo])
        out = cb_ref[:, cols] + bias * (cw_ref[0:1, cols] + cw_ref[1:2, cols] + cw_ref[2:3, cols])
        for k in range(HY_SHORT):
            out = out + p_ref[halo - 1 + k:halo - 1 + k + tm, :] * cw_ref[k:k + 1, cols]
        return out

    for c0 in range(0, d, cw_cols):
        x0_ref[:, c0:c0 + cw_cols] = conv(c0, p0_ref).astype(BF16)
        u_ref[:, c0:c0 + cw_cols] = (conv(d + c0, p1_ref) * conv(2 * d + c0, p2_ref)).astype(BF16)


def _hyena_inproj(x2, mod_l, mod_index, seq_len, g, w_bf, b, conv_w, conv_b):
    t_rows, d = x2.shape
    tm = min(512, seq_len)
    halo = 16
    tps = seq_len // tm
    nblk = t_rows // halo
    cw_cols = 256
    kern = functools.partial(_inproj_kernel, tm=tm, tiles_per_seq=tps, halo=halo, cw_cols=cw_cols)
    return pl.pallas_call(
        kern,
        grid=(t_rows // tm,),
        in_specs=[pl.BlockSpec((tm, d), lambda t: (t, 0)),
                  pl.BlockSpec((halo, d), lambda t: (jnp.maximum(t * (tm // halo) - 1, 0), 0)),
                  pl.BlockSpec((halo, d), lambda t: (jnp.minimum((t + 1) * (tm // halo), nblk - 1), 0)),
                  pl.BlockSpec((None, MOD_CHUNKS, d), lambda t: (mod_index(t, tps), 0, 0)),
                  _full((1, d)), _full((d, 3 * d)), _full((1, 3 * d)),
                  _full((HY_SHORT, 3 * d)), _full((1, 3 * d))],
        out_specs=[pl.BlockSpec((tm, d), lambda t: (t, 0)),
                   pl.BlockSpec((tm, d), lambda t: (t, 0))],
        out_shape=[jax.ShapeDtypeStruct((t_rows, d), BF16),
                   jax.ShapeDtypeStruct((t_rows, d), BF16)],
        scratch_shapes=[pltpu.VMEM((tm + 2 * halo, cw_cols), F32)] * 3,
        compiler_params=_params(1),
        name="hyena_inproj",
    )(x2, x2, x2, mod_l, g.reshape(1, d), w_bf, b.reshape(1, 3 * d), conv_w, conv_b.reshape(1, 3 * d))


def _filter_hidden_kernel(zt_ref, w1t_ref, b1_ref, f1_ref, w2t_ref, b2_ref, f2_ref, o_ref):
    h = jnp.sin(f1_ref[...] * (_dot3(w1t_ref[...], zt_ref[...]) + b1_ref[...]))
    o_ref[...] = jnp.sin(f2_ref[...] * (_dot3(w2t_ref[...], h) + b2_ref[...]))


def _filter_out_kernel(h_ref, w3_ref, dl_ref, o_ref, sum_ref, *, seq_len):
    i = pl.program_id(1)
    tr, td = o_ref.shape
    r = lax.broadcasted_iota(jnp.int32, (tr, td), 0) + i * tr
    pos = jnp.where(r < seq_len, r, 2 * seq_len - r)
    t = pos.astype(F32) / seq_len
    kc = _dot3(h_ref[...], w3_ref[...]) * jnp.exp(-t * dl_ref[...])
    kc = jnp.where(r == seq_len, 0.0, kc)
    o_ref[...] = kc

    @pl.when(i == 0)
    def _():
        sum_ref[...] = jnp.zeros_like(sum_ref)

    sum_ref[...] += jnp.sum(jnp.abs(kc), axis=0, keepdims=True)


def _filter_features(seq_len):
    f32 = jnp.float32
    pos_f = jnp.arange(seq_len, dtype=f32)
    pos_b = ((seq_len - jnp.arange(seq_len)) % seq_len).astype(f32)
    pos = jnp.concatenate([pos_f, pos_b])
    t = pos / seq_len
    bands = jnp.linspace(1e-4, HY_BANDS - 1, HY_BANDS, dtype=f32)
    ang = (2.0 * math.pi / seq_len) * pos[:, None] * bands[None, :]
    return jnp.concatenate([t[:, None], jnp.cos(ang), -jnp.sin(ang)], axis=-1)


def _hyena_filter(seq_len, d, f_w1, f_b1, f_fr1, f_w2, f_b2, f_fr2, f_w3):
    z = _filter_features(seq_len)
    emb = z.shape[1]
    hid = f_w1.shape[1]
    rows = 2 * seq_len
    tr = min(2048, seq_len)
    h2t = pl.pallas_call(
        _filter_hidden_kernel,
        grid=(rows // tr,),
        in_specs=[pl.BlockSpec((emb, tr), lambda i: (0, i)),
                  _full((hid, emb)), _full((hid, 1)), _full((hid, 1)),
                  _full((hid, hid)), _full((hid, 1)), _full((hid, 1))],
        out_specs=pl.BlockSpec((hid, tr), lambda i: (0, i)),
        out_shape=jax.ShapeDtypeStruct((hid, rows), F32),
        compiler_params=_params(1),
        name="hyena_filter_hidden",
    )(z.T, f_w1.T, f_b1.reshape(hid, 1), f_fr1.reshape(hid, 1), f_w2.T, f_b2.reshape(hid, 1),
      f_fr2.reshape(hid, 1))
    h2 = h2t.T
    deltas = jnp.abs(jnp.linspace(HY_MIN_DECAY, HY_MAX_DECAY, d, dtype=F32)).reshape(1, d)
    td = 1024
    nd = d // td
    tiles_per_half = seq_len // tr
    return pl.pallas_call(
        functools.partial(_filter_out_kernel, seq_len=seq_len),
        grid=(nd, rows // tr),
        in_specs=[pl.BlockSpec((tr, hid), lambda j, i: (i, 0)),
                  pl.BlockSpec((hid, td), lambda j, i: (0, (i // tiles_per_half) * nd + j)),
                  pl.BlockSpec((1, td), lambda j, i: (0, j))],
        out_specs=[pl.BlockSpec((tr, td), lambda j, i: (i, j)),
                   pl.BlockSpec((1, td), lambda j, i: (0, j))],
        out_shape=[jax.ShapeDtypeStruct((rows, d), F32), jax.ShapeDtypeStruct((1, d), F32)],
        compiler_params=_params(2),
        name="hyena_filter_out",
    )(h2, f_w3, deltas)


def _dft_tables(n1):
    n2 = FFT_N2
    n = n1 * n2
    h1 = n1 // 2
    k1 = np.concatenate([np.arange(0, h1 + 1), np.arange(1, h1)])
    is_im = np.concatenate([np.zeros(h1 + 1, bool), np.ones(h1 - 1, bool)])
    th = 2.0 * np.pi * np.outer(k1, np.arange(n1)) / n1
    f1 = np.where(is_im[:, None], -np.sin(th), np.cos(th))
    wgt = np.where((k1 == 0) | (k1 == h1), 1.0, 2.0)[:, None] / n
    f1inv = (np.where(is_im[:, None], -np.sin(th), np.cos(th)) * wgt).T
    m = np.arange(n2)
    k2 = np.arange(n2)

    def g(k1v):
        ph = 2.0 * np.pi * np.outer(k2 * n1 + k1v, m) / n
        return np.cos(ph), -np.sin(ph)

    mf = np.zeros((h1, 2 * n2, 2 * n2))
    for j in range(1, h1):
        gr, gi = g(j)
        mf[j] = np.block([[gr, -gi], [gi, gr]])
    g0r, g0i = g(0)
    ghr, ghi = g(h1)
    zz = np.zeros((n2, n2))
    m0 = np.block([[g0r, zz], [g0i, zz], [zz, ghr], [zz, ghi]])
    mfi = np.transpose(mf, (0, 2, 1))
    m0i = m0.T
    as32 = lambda a: jnp.asarray(a.astype(np.float32))
    return dict(f1=as32(f1), f1inv=as32(f1inv), mf=as32(mf), mfi=as32(mfi), m0=as32(m0), m0i=as32(m0i))


def _stage1_kernel(f_ref, x_ref, o_ref):
    xt = jnp.swapaxes(x_ref[...].astype(BF16), 0, 1)
    f = f_ref[...]
    out = jnp.stack([_dot(f, xt[t]).astype(o_ref.dtype) for t in range(xt.shape[0])], axis=0)
    o_ref[...] = jnp.swapaxes(out, 0, 1)


def _dft_stage1(f_bf, x4, out_dtype):
    bx, r, n2, d = x4.shape
    m = f_bf.shape[0]
    tt = (2 if x4.dtype == BF16 else 1) * BF16_SUBLANES
    return pl.pallas_call(
        _stage1_kernel,
        grid=(bx, n2 // tt),
        in_specs=[_full((m, r)), pl.BlockSpec((None, r, tt, d), lambda b, j: (b, 0, j, 0))],
        out_specs=pl.BlockSpec((None, m, tt, d), lambda b, j: (b, 0, j, 0)),
        out_shape=jax.ShapeDtypeStruct((bx, m, n2, d), out_dtype),
        compiler_params=_params(2),
        name="dft_stage1",
    )(f_bf, x4)


def _cmul(xr, xi, kr, ki):
    return xr * kr - xi * ki, xr * ki + xi * kr


def _stage2_kernel(a_ref, ak_ref, nrm_ref, mf_ref, mfi_ref, m0_ref, m0i_ref, o_ref):
    j = pl.program_id(0)
    n2 = FFT_N2
    nb = a_ref.shape[0]
    d = a_ref.shape[-1]
    ak = ak_ref[...].reshape(2 * n2, d)
    nrm = nrm_ref[...]

    @pl.when(j == 0)
    def _():
        k = _dot(m0_ref[...], ak) / nrm
        for b in range(nb):
            a = a_ref[b].reshape(2 * n2, d)
            x = _dot(m0_ref[...], a)
            y0r, y0i = _cmul(x[0:n2], x[n2:2 * n2], k[0:n2], k[n2:2 * n2])
            y1r, y1i = _cmul(x[2 * n2:3 * n2], x[3 * n2:4 * n2], k[2 * n2:3 * n2], k[3 * n2:4 * n2])
            y = jnp.concatenate([y0r, y0i, y1r, y1i], axis=0).astype(BF16)
            o_ref[b] = _dot(m0i_ref[...], y).reshape(2, n2, d).astype(o_ref.dtype)

    @pl.when(j > 0)
    def _():
        k = _dot(mf_ref[...], ak) / nrm
        for b in range(nb):
            a = a_ref[b].reshape(2 * n2, d)
            x = _dot(mf_ref[...], a)
            yr, yi = _cmul(x[0:n2], x[n2:2 * n2], k[0:n2], k[n2:2 * n2])
            y = jnp.concatenate([yr, yi], axis=0).astype(BF16)
            o_ref[b] = _dot(mfi_ref[...], y).reshape(2, n2, d).astype(o_ref.dtype)


def _dft_stage2(a5, ak, nrm, tabs):
    nb, _, h1, n2, d = a5.shape
    blk = pl.BlockSpec((nb, 2, None, n2, d), lambda j: (0, 0, j, 0, 0))
    return pl.pallas_call(
        _stage2_kernel,
        grid=(h1,),
        in_specs=[blk,
                  pl.BlockSpec((2, None, n2, d), lambda j: (0, j, 0, 0)),
                  _full((1, d)),
                  pl.BlockSpec((None, 2 * n2, 2 * n2), lambda j: (j, 0, 0)),
                  pl.BlockSpec((None, 2 * n2, 2 * n2), lambda j: (j, 0, 0)),
                  _full((4 * n2, 2 * n2)), _full((2 * n2, 4 * n2))],
        out_specs=blk,
        out_shape=jax.ShapeDtypeStruct(a5.shape, BF16),
        compiler_params=_params(1),
        name="dft_stage2",
    )(a5, ak, nrm, tabs["mf"].astype(BF16), tabs["mfi"].astype(BF16), tabs["m0"].astype(BF16),
      tabs["m0i"].astype(BF16))


def _long_conv(u, kc, nrm):
    b, seq_len, d = u.shape
    n2 = FFT_N2
    n1 = 2 * seq_len // n2
    h1 = n1 // 2
    tabs = _dft_tables(n1)
    f1 = tabs["f1"].astype(BF16)
    ak = _dft_stage1(f1, kc.reshape(1, n1, n2, d), BF16)
    a = _dft_stage1(f1[:, :h1], u.reshape(b, h1, n2, d), BF16)
    bv = _dft_stage2(a.reshape(b, 2, h1, n2, d), ak.reshape(2, h1, n2, d), nrm, tabs)
    y = _dft_stage1(tabs["f1inv"][:h1].astype(BF16), bv.reshape(b, n1, n2, d), BF16)
    return y.reshape(b, seq_len, d)


def _ctxconv_kernel(u_ref, kc_ref, nrm_ref, cf_ref, ci_ref, o_ref, *, seq_len):
    nf = 2 * seq_len
    kh, kl = _split_bf16(kc_ref[...])
    ksp = (_dot(cf_ref[...], kh) + _dot(cf_ref[...], kl)) / nrm_ref[...]
    kr, ki = ksp[0:nf], ksp[nf:2 * nf]
    for b in range(u_ref.shape[0]):
        x = _dot(cf_ref[:, 0:seq_len], u_ref[b])
        yr, yi = _cmul(x[0:nf], x[nf:2 * nf], kr, ki)
        y = jnp.concatenate([yr, yi], axis=0).astype(BF16)
        o_ref[b] = _dot(ci_ref[...], y).astype(o_ref.dtype)


def _short_seq_conv(u, kc, nrm):
    b, seq_len, d = u.shape
    nf = 2 * seq_len
    ph = 2.0 * np.pi * np.outer(np.arange(nf), np.arange(nf)) / nf
    cf = jnp.asarray(np.concatenate([np.cos(ph), -np.sin(ph)], axis=0).astype(np.float32))
    ci = jnp.asarray((np.concatenate([np.cos(ph), -np.sin(ph)], axis=1)[:seq_len] / nf).astype(np.float32))
    return pl.pallas_call(
        functools.partial(_ctxconv_kernel, seq_len=seq_len),
        grid=(1,),
        in_specs=[_full((b, seq_len, d)), _full((nf, d)), _full((1, d)), _full((2 * nf, nf)),
                  _full((seq_len, 2 * nf))],
        out_specs=_full((b, seq_len, d)),
        out_shape=jax.ShapeDtypeStruct((b, seq_len, d), BF16),
        compiler_params=_params(1),
        name="hyena_ctx_conv",
    )(u, kc, nrm, cf.astype(BF16), ci.astype(BF16))


def _mlp_body(x, mod_ref, g_ref, w1_ref, w2_ref, ff_chunk):
    h = _norm_mod(x, g_ref[...], mod_ref[3:4, :], mod_ref[4:5, :]).astype(BF16)
    acc = jnp.zeros(x.shape, F32)
    for c0 in range(0, w1_ref.shape[1], ff_chunk):
        a = jnp.maximum(_dot(h, w1_ref[:, c0:c0 + ff_chunk]), 0.0)
        acc = acc + _dot((a * a).astype(BF16), w2_ref[c0:c0 + ff_chunk, :])
    return x + mod_ref[5:6, :] * acc


def _hy_out_mlp_kernel(x_ref, x0_ref, u_ref, yc_ref, mod_ref, skip_ref, w_ref, b_ref, g_ref, w1_ref, w2_ref,
                       o_ref, *, ff_chunk):
    u = u_ref[...].astype(F32)
    y = x0_ref[...].astype(F32) * (yc_ref[...].astype(F32) + u * skip_ref[...])
    dx = _dot(y.astype(BF16), w_ref[...]) + b_ref[...]
    x = x_ref[...] + mod_ref[2:3, :] * dx
    o_ref[...] = _mlp_body(x, mod_ref, g_ref, w1_ref, w2_ref, ff_chunk)


def _resident(shape):
    nd = len(shape)
    return pl.BlockSpec(shape, lambda *_: (0,) * nd, pipeline_mode=pl.Buffered(1))


MLP_FF_CHUNK = 1024


def _hyena_out_mlp(x2, x0, u, yc, mod_l, mod_index, seq_len, skip, w_bf, b, g, w1_bf, w2_bf):
    t_rows, d = x2.shape
    ff = w1_bf.shape[1]
    tm = min(512, seq_len)
    tps = seq_len // tm
    row = pl.BlockSpec((tm, d), lambda t: (t, 0))
    return pl.pallas_call(
        functools.partial(_hy_out_mlp_kernel, ff_chunk=MLP_FF_CHUNK),
        grid=(t_rows // tm,),
        in_specs=[row, row, row, row,
                  pl.BlockSpec((None, MOD_CHUNKS, d), lambda t: (mod_index(t, tps), 0, 0)),
                  _full((1, d)), _resident((d, d)), _full((1, d)),
                  _full((1, d)), _resident((d, ff)), _resident((ff, d))],
        out_specs=row,
        out_shape=jax.ShapeDtypeStruct((t_rows, d), F32),
        compiler_params=_params(1),
        name="hyena_out_mlp",
    )(x2, x0, u, yc, mod_l, skip.reshape(1, d), w_bf, b.reshape(1, d), g.reshape(1, d), w1_bf, w2_bf)


def _mlp_kernel(x_ref, mod_ref, g_ref, w1_ref, w2_ref, o_ref, *, ff_chunk):
    o_ref[...] = _mlp_body(x_ref[...], mod_ref, g_ref, w1_ref, w2_ref, ff_chunk)


def _mlp(x2, mod_l, mod_index, seq_len, g, w1_bf, w2_bf):
    t_rows, d = x2.shape
    ff = w1_bf.shape[1]
    tm = min(1024, seq_len)
    tps = seq_len // tm
    row = pl.BlockSpec((tm, d), lambda t: (t, 0))
    return pl.pallas_call(
        functools.partial(_mlp_kernel, ff_chunk=MLP_FF_CHUNK),
        grid=(t_rows // tm,),
        in_specs=[row,
                  pl.BlockSpec((None, MOD_CHUNKS, d), lambda t: (mod_index(t, tps), 0, 0)),
                  _full((1, d)), _resident((d, ff)), _resident((ff, d))],
        out_specs=row,
        out_shape=jax.ShapeDtypeStruct((t_rows, d), F32),
        compiler_params=_params(1),
        name="mlp",
    )(x2, mod_l, g.reshape(1, d), w1_bf, w2_bf)


def _rope(v, cos, sin_signed, hi_half):
    partner = jnp.where(hi_half, pltpu.roll(v, 16, axis=1), pltpu.roll(v, LANES - 16, axis=1))
    return v * cos + partner * sin_signed


def _qkv_kernel(*refs, with_q, with_rope, q_width, kv_width):
    x_ref, mod_ref, g_ref, w_ref, b_ref, bd_ref, qn_ref, kn_ref = refs[:8]
    idx = 8
    if with_rope:
        cos_ref, sin_ref = refs[idx:idx + 2]
        idx += 2
    outs = refs[idx:]
    if with_q:
        q_ref, kt_ref, v_ref = outs
    else:
        kt_ref, v_ref = outs
    h = _norm_mod(x_ref[...], g_ref[...], mod_ref[0:1, :], mod_ref[1:2, :]).astype(BF16)
    tm = h.shape[0]
    if with_rope:
        cos = cos_ref[...]
        sin = sin_ref[...]
        hi_half = (lax.broadcasted_iota(jnp.int32, (tm, LANES), 1) & 16) != 0

    def head_norm(p, w, bd):
        sq = (p * p).astype(BF16)
        sw = bd.shape[0]
        ms = jnp.concatenate([_dot(sq[:, c0:c0 + sw], bd) for c0 in range(0, p.shape[1], sw)], axis=1)
        return p * lax.rsqrt(ms + EPS) * w

    def rotate(p):
        if not with_rope:
            return p
        return jnp.concatenate([_rope(p[:, c0:c0 + LANES], cos, sin, hi_half)
                                for c0 in range(0, p.shape[1], LANES)], axis=1)

    off = 0
    if with_q:
        q = _dot(h, w_ref[:, 0:q_width]) + b_ref[:, 0:q_width]
        q_ref[...] = rotate(head_norm(q, qn_ref[...], bd_ref[...]) * Q_SCALE_LOG2).astype(q_ref.dtype)
        off = q_width
    k = _dot(h, w_ref[:, off:off + kv_width]) + b_ref[:, off:off + kv_width]
    k = rotate(head_norm(k, kn_ref[...], bd_ref[...]))
    kt_ref[...] = k.T.astype(kt_ref.dtype)
    v = _dot(h, w_ref[:, off + kv_width:off + 2 * kv_width]) + b_ref[:, off + kv_width:off + 2 * kv_width]
    v_ref[...] = v.astype(v_ref.dtype)


def _rope_tables(seq_len):
    rows = seq_len // GRID_W
    row = jnp.repeat(jnp.arange(rows, dtype=F32), GRID_W)
    col = jnp.tile(jnp.arange(GRID_W, dtype=F32), rows)
    inv = ROPE_BASE ** (-jnp.arange(ROPE_NFREQ, dtype=F32) / ROPE_NFREQ)
    ar = row[:, None] * inv[None, :]
    ac = col[:, None] * inv[None, :]
    cos = jnp.concatenate([jnp.cos(ar), jnp.cos(ar), jnp.cos(ac), jnp.cos(ac)], axis=-1)
    sin = jnp.concatenate([-jnp.sin(ar), jnp.sin(ar), -jnp.sin(ac), jnp.sin(ac)], axis=-1)
    return jnp.tile(cos, (1, LANES // HEAD_DIM)), jnp.tile(sin, (1, LANES // HEAD_DIM))


def _qkv_proj(x2, mod_l, mod_index, seq_len, g, w_bf, b, qn, kn, with_q, rope):
    t_rows, d = x2.shape
    q_width = N_HEADS * HEAD_DIM
    kv_width = N_KV_HEADS * HEAD_DIM
    width = w_bf.shape[1]
    tm = min(1024, seq_len)
    tps = seq_len // tm
    slab_heads = V7X_MXU_DIM // HEAD_DIM
    bd = jnp.asarray(np.kron(np.eye(slab_heads), np.full((HEAD_DIM, HEAD_DIM), 1.0 / HEAD_DIM)).astype(np.float32))
    row = lambda w: pl.BlockSpec((tm, w), lambda t: (t, 0))
    in_specs = [row(d),
                pl.BlockSpec((None, MOD_CHUNKS, d), lambda t: (mod_index(t, tps), 0, 0)),
                _full((1, d)), _full((d, width)), _full((1, width)), _full((V7X_MXU_DIM, V7X_MXU_DIM)),
                _full((1, q_width)), _full((1, kv_width))]
    args = [x2, mod_l, g.reshape(1, d), w_bf, b.reshape(1, width), bd.astype(BF16),
            jnp.tile(qn, N_HEADS).reshape(1, q_width), jnp.tile(kn, N_KV_HEADS).reshape(1, kv_width)]
    if rope is not None:
        in_specs += [pl.BlockSpec((tm, LANES), lambda t: (t % tps, 0))] * 2
        args += list(rope)
    out_specs = [pl.BlockSpec((kv_width, tm), lambda t: (0, t)), row(kv_width)]
    out_shape = [jax.ShapeDtypeStruct((kv_width, t_rows), BF16), jax.ShapeDtypeStruct((t_rows, kv_width), BF16)]
    if with_q:
        out_specs = [row(q_width)] + out_specs
        out_shape = [jax.ShapeDtypeStruct((t_rows, q_width), BF16)] + out_shape
    return pl.pallas_call(
        functools.partial(_qkv_kernel, with_q=with_q, with_rope=rope is not None,
                          q_width=q_width, kv_width=kv_width),
        grid=(t_rows // tm,),
        in_specs=in_specs, out_specs=out_specs, out_shape=out_shape,
        compiler_params=_params(1),
        name="qkv_proj",
    )(*args)


def _attn_kernel(sink_ref, x_ref, q_ref, ktp_ref, ktm_ref, ktn_ref, ktc_ref, vp_ref, vm_ref, vn_ref, vc_ref,
                 mod_ref, w_ref, b_ref, o_ref, att_ref, *, tq, n_qt):
    i = pl.program_id(1)
    lc = vc_ref.shape[0]
    qb = min(ATT_QB, tq)
    n_sub = qb // WINDOW
    nw = qb + 2 * WINDOW
    nk = nw + lc
    sw = 3 * WINDOW
    sk = sw + lc
    n_rb = tq // qb
    r = lax.broadcasted_iota(jnp.int32, (WINDOW, WINDOW), 0)
    c = lax.broadcasted_iota(jnp.int32, (WINDOW, WINDOW), 1)
    before_bias = jnp.where(c >= r, 0.0, NEG_BIG)
    after_bias = jnp.where(c <= r, 0.0, NEG_BIG)
    first_before = jnp.where(jnp.logical_and(c >= r, i > 0), 0.0, NEG_BIG)
    last_after = jnp.where(jnp.logical_and(c <= r, i < n_qt - 1), 0.0, NEG_BIG)

    def masked_scores(raw, rb, j):
        rows = slice(j * WINDOW, (j + 1) * WINDOW)
        lo = j * WINDOW
        bb = first_before if (rb == 0 and j == 0) else before_bias
        ba = last_after if (rb == n_rb - 1 and j == n_sub - 1) else after_bias
        return jnp.concatenate([raw[rows, lo:lo + WINDOW] + bb,
                                raw[rows, lo + WINDOW:lo + 2 * WINDOW],
                                raw[rows, lo + 2 * WINDOW:lo + sw] + ba,
                                raw[rows, nw:nk]], axis=1)

    zeros = jnp.zeros((HEAD_DIM, nk), BF16)
    for hk in range(N_KV_HEADS):
        hs = slice(hk * HEAD_DIM, (hk + 1) * HEAD_DIM)
        ktw = jnp.concatenate([ktp_ref[hs, :], ktm_ref[hs, :], ktn_ref[hs, :]], axis=1)
        vw = jnp.concatenate([vp_ref[:, hs], vm_ref[:, hs], vn_ref[:, hs]], axis=0)
        ktc = ktc_ref[hs, :]
        vcx = vc_ref[:, hs]
        heads = [hk * GROUP + gq for gq in range(GROUP)]
        for rb in range(n_rb):
            rows = slice(rb * qb, (rb + 1) * qb)
            kk = jnp.concatenate([ktw[:, rb * qb:rb * qb + nw], ktc], axis=1)
            vv = jnp.concatenate([vw[rb * qb:rb * qb + nw], vcx], axis=0)
            kk_pair = (jnp.concatenate([kk, zeros], axis=0), jnp.concatenate([zeros, kk], axis=0))
            raw = [_dot(q_ref[rows, (hd // 2) * LANES:(hd // 2 + 1) * LANES], kk_pair[hd % 2]) for hd in heads]
            chains = [(gq, j) for gq in range(GROUP) for j in range(n_sub)]
            ss = [masked_scores(raw[gq], rb, j) for gq, j in chains]
            ms = [jnp.maximum(jnp.max(s, axis=-1, keepdims=True), sink_ref[heads[gq]])
                  for s, (gq, j) in zip(ss, chains)]
            ps = [jnp.exp2(s - m) for s, m in zip(ss, ms)]
            dens = [jnp.sum(p, axis=-1, keepdims=True) + jnp.exp2(sink_ref[heads[gq]] - m)
                    for p, m, (gq, j) in zip(ps, ms, chains)]
            for gq, hd in enumerate(heads):
                full = []
                for j in range(n_sub):
                    p = ps[gq * n_sub + j].astype(BF16)
                    parts = [p[:, 0:sw], p[:, sw:sk]]
                    if j > 0:
                        parts.insert(0, jnp.zeros((WINDOW, j * WINDOW), BF16))
                    if nw - sw - j * WINDOW > 0:
                        parts.insert(-1, jnp.zeros((WINDOW, nw - sw - j * WINDOW), BF16))
                    full.append(jnp.concatenate(parts, axis=1))
                den = jnp.concatenate(dens[gq * n_sub:(gq + 1) * n_sub], axis=0)
                o = _dot(jnp.concatenate(full, axis=0), vv) / den
                att_ref[rows, hd * HEAD_DIM:(hd + 1) * HEAD_DIM] = o.astype(BF16)
    dx = _dot(att_ref[...], w_ref[...]) + b_ref[...]
    o_ref[...] = x_ref[...] + mod_ref[2:3, :] * dx


def _window_attention(x2, q, kt, v, ktc, vc, sink, mod_l, batch, seq_len, w_bf, b):
    t_rows, d = x2.shape
    kvw = v.shape[1]
    lc = vc.shape[0] // batch
    tq = min(512, seq_len)
    n_qt = seq_len // tq
    wpt = tq // WINDOW
    nwb = t_rows // WINDOW
    tile = lambda bi, i: bi * n_qt + i
    prev_blk = lambda bi, i: jnp.maximum(tile(bi, i) * wpt - 1, 0)
    next_blk = lambda bi, i: jnp.minimum((tile(bi, i) + 1) * wpt, nwb - 1)
    qrow = lambda w: pl.BlockSpec((tq, w), lambda bi, i: (tile(bi, i), 0))
    return pl.pallas_call(
        functools.partial(_attn_kernel, tq=tq, n_qt=n_qt),
        grid=(batch, n_qt),
        in_specs=[pl.BlockSpec(memory_space=pltpu.SMEM),
                  qrow(d), qrow(q.shape[1]),
                  pl.BlockSpec((kvw, WINDOW), lambda bi, i: (0, prev_blk(bi, i))),
                  pl.BlockSpec((kvw, tq), lambda bi, i: (0, tile(bi, i))),
                  pl.BlockSpec((kvw, WINDOW), lambda bi, i: (0, next_blk(bi, i))),
                  pl.BlockSpec((kvw, lc), lambda bi, i: (0, bi)),
                  pl.BlockSpec((WINDOW, kvw), lambda bi, i: (prev_blk(bi, i), 0)),
                  qrow(kvw),
                  pl.BlockSpec((WINDOW, kvw), lambda bi, i: (next_blk(bi, i), 0)),
                  pl.BlockSpec((lc, kvw), lambda bi, i: (bi, 0)),
                  pl.BlockSpec((None, MOD_CHUNKS, d), lambda bi, i: (bi, 0, 0)),
                  _resident((d, d)), pl.BlockSpec((1, d), lambda bi, i: (0, 0))],
        out_specs=qrow(d),
        out_shape=jax.ShapeDtypeStruct((t_rows, d), F32),
        scratch_shapes=[pltpu.VMEM((tq, d), BF16)],
        compiler_params=_params(2),
        name="window_attention",
    )(sink, x2, q, kt, kt, kt, ktc, v, v, v, vc, mod_l, w_bf, b.reshape(1, d))


def _latent_mod_index(t, tiles_per_seq):
    return t // tiles_per_seq


def kernel(x, c, ctx, c_ctx, mod_w, mod_b, norm1_w, norm2_w, mlp_w1, mlp_w2, hy_w_in, hy_b_in, hy_conv_w,
           hy_conv_b, hy_f_w1, hy_f_b1, hy_f_freq1, hy_f_w2, hy_f_b2, hy_f_freq2, hy_f_w3, hy_skip, hy_w_out,
           hy_b_out, at_w_qkv, at_b_qkv, at_q_norm, at_k_norm, at_sink, at_w_out, at_b_out):
    batch, seq_len, d = x.shape
    lc = ctx.shape[1]
    ctx_row = batch

    def ctx_mod_index(t, tiles_per_seq):
        return ctx_row

    cond_rows = -(-(batch + 1) // 8) * 8
    cc = jnp.zeros((cond_rows, d), F32).at[:batch].set(c).at[batch].set(c_ctx)
    mod = _modulation(cc, mod_w, mod_b)

    x2 = x.reshape(batch * seq_len, d)
    c2 = ctx.reshape(batch * lc, d)

    w_in = hy_w_in[0].astype(BF16)
    w_out = hy_w_out[0].astype(BF16)
    w1 = mlp_w1[0].astype(BF16)
    w2 = mlp_w2[0].astype(BF16)
    fpar = (hy_f_w1[0], hy_f_b1[0], hy_f_freq1[0], hy_f_w2[0], hy_f_b2[0], hy_f_freq2[0], hy_f_w3[0])
    def hyena_layer(tok, n_seq, idx_fn, conv):
        x0, u = _hyena_inproj(tok, mod[0], idx_fn, n_seq, norm1_w[0], w_in, hy_b_in[0], hy_conv_w[0],
                              hy_conv_b[0])
        kc, nrm = _hyena_filter(n_seq, d, *fpar)
        yc = conv(u.reshape(batch, n_seq, d), kc, nrm).reshape(batch * n_seq, d)
        return _hyena_out_mlp(tok, x0, u, yc, mod[0], idx_fn, n_seq, hy_skip[0], w_out, hy_b_out[0],
                              norm2_w[0], w1, w2)

    x2 = hyena_layer(x2, seq_len, _latent_mod_index, _long_conv)
    c2 = hyena_layer(c2, lc, ctx_mod_index, _short_seq_conv)

    w_qkv = at_w_qkv[0].astype(BF16)
    q_width = N_HEADS * HEAD_DIM
    rope = _rope_tables(seq_len)
    q, kt, v = _qkv_proj(x2, mod[1], _latent_mod_index, seq_len, norm1_w[1], w_qkv, at_b_qkv[0],
                         at_q_norm[0], at_k_norm[0], True, rope)
    kt_ctx, v_ctx = _qkv_proj(c2, mod[1], ctx_mod_index, lc, norm1_w[1], w_qkv[:, q_width:],
                              at_b_qkv[0][q_width:], at_q_norm[0], at_k_norm[0], False, None)
    x2 = _window_attention(x2, q, kt, v, kt_ctx, v_ctx, at_sink[0].astype(F32) * LOG2_E, mod[1], batch,
                           seq_len, at_w_out[0].astype(BF16), at_b_out[0])
    x2 = _mlp(x2, mod[1], _latent_mod_index, seq_len, norm2_w[1], mlp_w1[1].astype(BF16),
              mlp_w2[1].astype(BF16))
    return x2.reshape(batch, seq_len, d)
```

```python
import functools
import math

import numpy as np
import jax
import jax.numpy as jnp
from jax import lax
from jax.experimental import pallas as pl
from jax.experimental.pallas import tpu as pltpu

F32 = jnp.float32
BF16 = jnp.bfloat16

EPS = 1e-6
MOD_CHUNKS = 6
GRID_W = 64
HY_SHORT = 3
HY_BANDS = 16
HY_DECAY_TARGET = 1e-2
HY_MAX_DECAY = math.log(HY_DECAY_TARGET) / 0.3
HY_MIN_DECAY = math.log(HY_DECAY_TARGET) / 1.5
N_HEADS = 16
N_KV_HEADS = 4
GROUP = N_HEADS // N_KV_HEADS
HEAD_DIM = 64
WINDOW = 128
ROPE_BASE = 10000.0
ROPE_NFREQ = HEAD_DIM // 4
LOG2_E = math.log2(math.e)
Q_SCALE_LOG2 = HEAD_DIM ** -0.5 * LOG2_E

V7X_VMEM_BYTES = 64 * 1024 * 1024
V7X_MXU_DIM = 256
VMEM_LIMIT = V7X_VMEM_BYTES - 8 * 1024 * 1024
LANES = 128
BF16_SUBLANES = 16
FFT_N2 = 128
NEG_BIG = -1e30
ATT_QB = 256


def _params(n_axes, flags=None):
    return pltpu.CompilerParams(dimension_semantics=("arbitrary",) * n_axes,
                                vmem_limit_bytes=VMEM_LIMIT, flags=flags)


def _split_bf16(a):
    hi = a.astype(BF16)
    lo = (a - hi.astype(F32)).astype(BF16)
    return hi, lo


def _dot(a, b):
    return jnp.dot(a, b, preferred_element_type=F32)


def _dot3(a, b):
    ah, al = _split_bf16(a)
    bh, bl = _split_bf16(b)
    return _dot(ah, bh) + (_dot(ah, bl) + _dot(al, bh))


def _norm_mod(x, g, shift, scale):
    ms = jnp.mean(x * x, axis=-1, keepdims=True)
    return x * lax.rsqrt(ms + EPS) * (g * (1.0 + scale)) + shift


def _full(shape):
    nd = len(shape)
    return pl.BlockSpec(shape, lambda *_: (0,) * nd)


def _mod_kernel(c_ref, w_ref, b_ref, o_ref):
    c = c_ref[...]
    s = c * jax.nn.sigmoid(c)
    o_ref[...] = _dot3(s, w_ref[...]) + b_ref[...]


def _modulation(cc, mod_w, mod_b):
    depth, d, n = mod_w.shape
    r = cc.shape[0]
    tn = 1536
    out = pl.pallas_call(
        _mod_kernel,
        grid=(depth, n // tn),
        in_specs=[pl.BlockSpec((r, d), lambda i, j: (0, 0)),
                  pl.BlockSpec((None, d, tn), lambda i, j: (i, 0, j)),
                  pl.BlockSpec((None, 1, tn), lambda i, j: (i, 0, j))],
        out_specs=pl.BlockSpec((None, r, tn), lambda i, j: (i, 0, j)),
        out_shape=jax.ShapeDtypeStruct((depth, r, n), F32),
        compiler_params=_params(2),
        name="modulation",
    )(cc, mod_w, mod_b.reshape(depth, 1, n))
    return out.reshape(depth, r, MOD_CHUNKS, d)


def _inproj_kernel(x_ref, xp_ref, xn_ref, mod_ref, g_ref, w_ref, b_ref, cw_ref, cb_ref,
                   x0_ref, u_ref, p0_ref, p1_ref, p2_ref, *, tm, tiles_per_seq, halo, cw_cols):
    t = pl.program_id(0)
    first = (t % tiles_per_seq) == 0
    last = (t % tiles_per_seq) == tiles_per_seq - 1
    g = g_ref[...]
    shift = mod_ref[0:1, :]
    scale = mod_ref[1:2, :]

    def nm(v):
        return _norm_mod(v, g, shift, scale).astype(BF16)

    h = jnp.concatenate([nm(xp_ref[...]), nm(x_ref[...]), nm(xn_ref[...])], axis=0)
    d = x_ref.shape[1]
    row = lax.broadcasted_iota(jnp.int32, (halo, cw_cols), 0)
    kill_top = jnp.logical_and(first, row == halo - 1)
    kill_bot = jnp.logical_and(last, row == 0)

    def conv(col0, p_ref):
        cols = slice(col0, col0 + cw_cols)
        bias = b_ref[:, cols]
        p = _dot(h, w_ref[:, cols])
        p_ref[0:halo, :] = jnp.where(kill_top, -bias, p[0:halo])
        p_ref[halo:tm + halo, :] = p[halo:tm + halo]
        p_ref[tm + halo:tm + 2 * halo, :] = jnp.where(kill_bot, -bias, p[tm + halo:tm + 2 * halo])
        out = cb_ref[:, cols] + bias * (cw_ref[0:1, cols] + cw_ref[1:2, cols] + cw_ref[2:3, cols])
        for k in range(HY_SHORT):
            out = out + p_ref[halo - 1 + k:halo - 1 + k + tm, :] * cw_ref[k:k + 1, cols]
        return out

    for c0 in range(0, d, cw_cols):
        x0_ref[:, c0:c0 + cw_cols] = conv(c0, p0_ref).astype(BF16)
        u_ref[:, c0:c0 + cw_cols] = (conv(d + c0, p1_ref) * conv(2 * d + c0, p2_ref)).astype(BF16)


def _hyena_inproj(x2, mod_l, mod_index, seq_len, g, w_bf, b, conv_w, conv_b):
    t_rows, d = x2.shape
    tm = min(512, seq_len)
    halo = 16
    tps = seq_len // tm
    nblk = t_rows // halo
    cw_cols = 256
    kern = functools.partial(_inproj_kernel, tm=tm, tiles_per_seq=tps, halo=halo, cw_cols=cw_cols)
    return pl.pallas_call(
        kern,
        grid=(t_rows // tm,),
        in_specs=[pl.BlockSpec((tm, d), lambda t: (t, 0)),
                  pl.BlockSpec((halo, d), lambda t: (jnp.maximum(t * (tm // halo) - 1, 0), 0)),
                  pl.BlockSpec((halo, d), lambda t: (jnp.minimum((t + 1) * (tm // halo), nblk - 1), 0)),
                  pl.BlockSpec((None, MOD_CHUNKS, d), lambda t: (mod_index(t, tps), 0, 0)),
                  _full((1, d)), _full((d, 3 * d)), _full((1, 3 * d)),
                  _full((HY_SHORT, 3 * d)), _full((1, 3 * d))],
        out_specs=[pl.BlockSpec((tm, d), lambda t: (t, 0)),
                   pl.BlockSpec((tm, d), lambda t: (t, 0))],
        out_shape=[jax.ShapeDtypeStruct((t_rows, d), BF16),
                   jax.ShapeDtypeStruct((t_rows, d), BF16)],
        scratch_shapes=[pltpu.VMEM((tm + 2 * halo, cw_cols), F32)] * 3,
        compiler_params=_params(1),
        name="hyena_inproj",
    )(x2, x2, x2, mod_l, g.reshape(1, d), w_bf, b.reshape(1, 3 * d), conv_w, conv_b.reshape(1, 3 * d))


def _filter_hidden_kernel(zt_ref, w1t_ref, b1_ref, f1_ref, w2t_ref, b2_ref, f2_ref, o_ref):
    h = jnp.sin(f1_ref[...] * (_dot3(w1t_ref[...], zt_ref[...]) + b1_ref[...]))
    o_ref[...] = jnp.sin(f2_ref[...] * (_dot3(w2t_ref[...], h) + b2_ref[...]))


def _filter_out_kernel(h_ref, w3_ref, dl_ref, o_ref, sum_ref, *, seq_len):
    i = pl.program_id(1)
    tr, td = o_ref.shape
    r = lax.broadcasted_iota(jnp.int32, (tr, td), 0) + i * tr
    pos = jnp.where(r < seq_len, r, 2 * seq_len - r)
    t = pos.astype(F32) / seq_len
    kc = _dot3(h_ref[...], w3_ref[...]) * jnp.exp(-t * dl_ref[...])
    kc = jnp.where(r == seq_len, 0.0, kc)
    o_ref[...] = kc

    @pl.when(i == 0)
    def _():
        sum_ref[...] = jnp.zeros_like(sum_ref)

    sum_ref[...] += jnp.sum(jnp.abs(kc), axis=0, keepdims=True)


def _filter_features(seq_len):
    f32 = jnp.float32
    pos_f = jnp.arange(seq_len, dtype=f32)
    pos_b = ((seq_len - jnp.arange(seq_len)) % seq_len).astype(f32)
    pos = jnp.concatenate([pos_f, pos_b])
    t = pos / seq_len
    bands = jnp.linspace(1e-4, HY_BANDS - 1, HY_BANDS, dtype=f32)
    ang = (2.0 * math.pi / seq_len) * pos[:, None] * bands[None, :]
    return jnp.concatenate([t[:, None], jnp.cos(ang), -jnp.sin(ang)], axis=-1)


def _hyena_filter(seq_len, d, f_w1, f_b1, f_fr1, f_w2, f_b2, f_fr2, f_w3):
    z = _filter_features(seq_len)
    emb = z.shape[1]
    hid = f_w1.shape[1]
    rows = 2 * seq_len
    tr = min(2048, seq_len)
    h2t = pl.pallas_call(
        _filter_hidden_kernel,
        grid=(rows // tr,),
        in_specs=[pl.BlockSpec((emb, tr), lambda i: (0, i)),
                  _full((hid, emb)), _full((hid, 1)), _full((hid, 1)),
                  _full((hid, hid)), _full((hid, 1)), _full((hid, 1))],
        out_specs=pl.BlockSpec((hid, tr), lambda i: (0, i)),
        out_shape=jax.ShapeDtypeStruct((hid, rows), F32),
        compiler_params=_params(1),
        name="hyena_filter_hidden",
    )(z.T, f_w1.T, f_b1.reshape(hid, 1), f_fr1.reshape(hid, 1), f_w2.T, f_b2.reshape(hid, 1),
      f_fr2.reshape(hid, 1))
    h2 = h2t.T
    deltas = jnp.abs(jnp.linspace(HY_MIN_DECAY, HY_MAX_DECAY, d, dtype=F32)).reshape(1, d)
    td = 1024
    nd = d // td
    tiles_per_half = seq_len // tr
    return pl.pallas_call(
        functools.partial(_filter_out_kernel, seq_len=seq_len),
        grid=(nd, rows // tr),
        in_specs=[pl.BlockSpec((tr, hid), lambda j, i: (i, 0)),
                  pl.BlockSpec((hid, td), lambda j, i: (0, (i // tiles_per_half) * nd + j)),
                  pl.BlockSpec((1, td), lambda j, i: (0, j))],
        out_specs=[pl.BlockSpec((tr, td), lambda j, i: (i, j)),
                   pl.BlockSpec((1, td), lambda j, i: (0, j))],
        out_shape=[jax.ShapeDtypeStruct((rows, d), F32), jax.ShapeDtypeStruct((1, d), F32)],
        compiler_params=_params(2),
        name="hyena_filter_out",
    )(h2, f_w3, deltas)


def _dft_tables(n1):
    n2 = FFT_N2
    n = n1 * n2
    h1 = n1 // 2
    k1 = np.concatenate([np.arange(0, h1 + 1), np.arange(1, h1)])
    is_im = np.concatenate([np.zeros(h1 + 1, bool), np.ones(h1 - 1, bool)])
    th = 2.0 * np.pi * np.outer(k1, np.arange(n1)) / n1
    f1 = np.where(is_im[:, None], -np.sin(th), np.cos(th))
    wgt = np.where((k1 == 0) | (k1 == h1), 1.0, 2.0)[:, None] / n
    f1inv = (np.where(is_im[:, None], -np.sin(th), np.cos(th)) * wgt).T
    m = np.arange(n2)
    k2 = np.arange(n2)

    def g(k1v):
        ph = 2.0 * np.pi * np.outer(k2 * n1 + k1v, m) / n
        return np.cos(ph), -np.sin(ph)

    mf = np.zeros((h1, 2 * n2, 2 * n2))
    for j in range(1, h1):
        gr, gi = g(j)
        mf[j] = np.block([[gr, -gi], [gi, gr]])
    g0r, g0i = g(0)
    ghr, ghi = g(h1)
    zz = np.zeros((n2, n2))
    m0 = np.block([[g0r, zz], [g0i, zz], [zz, ghr], [zz, ghi]])
    mfi = np.transpose(mf, (0, 2, 1))
    m0i = m0.T
    as32 = lambda a: jnp.asarray(a.astype(np.float32))
    return dict(f1=as32(f1), f1inv=as32(f1inv), mf=as32(mf), mfi=as32(mfi), m0=as32(m0), m0i=as32(m0i))


def _stage1_kernel(f_ref, x_ref, o_ref):
    xt = jnp.swapaxes(x_ref[...].astype(BF16), 0, 1)
    f = f_ref[...]
    out = jnp.stack([_dot(f, xt[t]).astype(o_ref.dtype) for t in range(xt.shape[0])], axis=0)
    o_ref[...] = jnp.swapaxes(out, 0, 1)


def _dft_stage1(f_bf, x4, out_dtype):
    bx, r, n2, d = x4.shape
    m = f_bf.shape[0]
    tt = (2 if x4.dtype == BF16 else 1) * BF16_SUBLANES
    return pl.pallas_call(
        _stage1_kernel,
        grid=(bx, n2 // tt),
        in_specs=[_full((m, r)), pl.BlockSpec((None, r, tt, d), lambda b, j: (b, 0, j, 0))],
        out_specs=pl.BlockSpec((None, m, tt, d), lambda b, j: (b, 0, j, 0)),
        out_shape=jax.ShapeDtypeStruct((bx, m, n2, d), out_dtype),
        compiler_params=_params(2),
        name="dft_stage1",
    )(f_bf, x4)


def _cmul(xr, xi, kr, ki):
    return xr * kr - xi * ki, xr * ki + xi * kr


def _stage2_kernel(a_ref, ak_ref, nrm_ref, mf_ref, mfi_ref, m0_ref, m0i_ref, o_ref):
    j = pl.program_id(0)
    n2 = FFT_N2
    nb = a_ref.shape[0]
    d = a_ref.shape[-1]
    ak = ak_ref[...].reshape(2 * n2, d)
    nrm = nrm_ref[...]

    @pl.when(j == 0)
    def _():
        k = _dot(m0_ref[...], ak) / nrm
        for b in range(nb):
            a = a_ref[b].reshape(2 * n2, d)
            x = _dot(m0_ref[...], a)
            y0r, y0i = _cmul(x[0:n2], x[n2:2 * n2], k[0:n2], k[n2:2 * n2])
            y1r, y1i = _cmul(x[2 * n2:3 * n2], x[3 * n2:4 * n2], k[2 * n2:3 * n2], k[3 * n2:4 * n2])
            y = jnp.concatenate([y0r, y0i, y1r, y1i], axis=0).astype(BF16)
            o_ref[b] = _dot(m0i_ref[...], y).reshape(2, n2, d).astype(o_ref.dtype)

    @pl.when(j > 0)
    def _():
        kt = (_dot(mf_ref[...], ak) / nrm).T
        for b in range(nb):
            a = a_ref[b].reshape(2 * n2, d)
            xt = lax.dot_general(a, mfi_ref[...], (((0,), (0,)), ((), ())), preferred_element_type=F32)
            yr, yi = _cmul(xt[:, 0:n2], xt[:, n2:2 * n2], kt[:, 0:n2], kt[:, n2:2 * n2])
            yt = jnp.concatenate([yr, yi], axis=1).astype(BF16)
            o_ref[b] = _dot(yt, mf_ref[...]).T.reshape(2, n2, d).astype(o_ref.dtype)


def _dft_stage2(a5, ak, nrm, tabs):
    nb, _, h1, n2, d = a5.shape
    blk = pl.BlockSpec((nb, 2, None, n2, d), lambda j: (0, 0, j, 0, 0))
    return pl.pallas_call(
        _stage2_kernel,
        grid=(h1,),
        in_specs=[blk,
                  pl.BlockSpec((2, None, n2, d), lambda j: (0, j, 0, 0)),
                  _full((1, d)),
                  pl.BlockSpec((None, 2 * n2, 2 * n2), lambda j: (j, 0, 0)),
                  pl.BlockSpec((None, 2 * n2, 2 * n2), lambda j: (j, 0, 0)),
                  _full((4 * n2, 2 * n2)), _full((2 * n2, 4 * n2))],
        out_specs=blk,
        out_shape=jax.ShapeDtypeStruct(a5.shape, BF16),
        compiler_params=_params(1),
        name="dft_stage2",
    )(a5, ak, nrm, tabs["mf"].astype(BF16), tabs["mfi"].astype(BF16), tabs["m0"].astype(BF16),
      tabs["m0i"].astype(BF16))


def _long_conv(u, kc, nrm):
    b, seq_len, d = u.shape
    n2 = FFT_N2
    n1 = 2 * seq_len // n2
    h1 = n1 // 2
    tabs = _dft_tables(n1)
    f1 = tabs["f1"].astype(BF16)
    ak = _dft_stage1(f1, kc.reshape(1, n1, n2, d), BF16)
    a = _dft_stage1(f1[:, :h1], u.reshape(b, h1, n2, d), BF16)
    bv = _dft_stage2(a.reshape(b, 2, h1, n2, d), ak.reshape(2, h1, n2, d), nrm, tabs)
    y = _dft_stage1(tabs["f1inv"][:h1].astype(BF16), bv.reshape(b, n1, n2, d), BF16)
    return y.reshape(b, seq_len, d)


def _ctxconv_kernel(u_ref, kc_ref, nrm_ref, cf_ref, ci_ref, o_ref, *, seq_len):
    nf = 2 * seq_len
    kh, kl = _split_bf16(kc_ref[...])
    ksp = (_dot(cf_ref[...], kh) + _dot(cf_ref[...], kl)) / nrm_ref[...]
    kr, ki = ksp[0:nf], ksp[nf:2 * nf]
    for b in range(u_ref.shape[0]):
        x = _dot(cf_ref[:, 0:seq_len], u_ref[b])
        yr, yi = _cmul(x[0:nf], x[nf:2 * nf], kr, ki)
        y = jnp.concatenate([yr, yi], axis=0).astype(BF16)
        o_ref[b] = _dot(ci_ref[...], y).astype(o_ref.dtype)


def _short_seq_conv(u, kc, nrm):
    b, seq_len, d = u.shape
    nf = 2 * seq_len
    ph = 2.0 * np.pi * np.outer(np.arange(nf), np.arange(nf)) / nf
    cf = jnp.asarray(np.concatenate([np.cos(ph), -np.sin(ph)], axis=0).astype(np.float32))
    ci = jnp.asarray((np.concatenate([np.cos(ph), -np.sin(ph)], axis=1)[:seq_len] / nf).astype(np.float32))
    return pl.pallas_call(
        functools.partial(_ctxconv_kernel, seq_len=seq_len),
        grid=(1,),
        in_specs=[_full((b, seq_len, d)), _full((nf, d)), _full((1, d)), _full((2 * nf, nf)),
                  _full((seq_len, 2 * nf))],
        out_specs=_full((b, seq_len, d)),
        out_shape=jax.ShapeDtypeStruct((b, seq_len, d), BF16),
        compiler_params=_params(1),
        name="hyena_ctx_conv",
    )(u, kc, nrm, cf.astype(BF16), ci.astype(BF16))


def _mlp_body(x, mod_ref, g_ref, w1_ref, w2_ref, ff_chunk):
    h = _norm_mod(x, g_ref[...], mod_ref[3:4, :], mod_ref[4:5, :]).astype(BF16)
    acc = jnp.zeros(x.shape, F32)
    for c0 in range(0, w1_ref.shape[1], ff_chunk):
        a = jnp.maximum(_dot(h, w1_ref[:, c0:c0 + ff_chunk]), 0.0)
        acc = acc + _dot((a * a).astype(BF16), w2_ref[c0:c0 + ff_chunk, :])
    return x + mod_ref[5:6, :] * acc


def _hy_out_mlp_kernel(x_ref, x0_ref, u_ref, yc_ref, mod_ref, skip_ref, w_ref, b_ref, g_ref, w1_ref, w2_ref,
                       o_ref, *, ff_chunk):
    u = u_ref[...].astype(F32)
    y = x0_ref[...].astype(F32) * (yc_ref[...].astype(F32) + u * skip_ref[...])
    dx = _dot(y.astype(BF16), w_ref[...]) + b_ref[...]
    x = x_ref[...] + mod_ref[2:3, :] * dx
    o_ref[...] = _mlp_body(x, mod_ref, g_ref, w1_ref, w2_ref, ff_chunk)


def _resident(shape):
    nd = len(shape)
    return pl.BlockSpec(shape, lambda *_: (0,) * nd, pipeline_mode=pl.Buffered(1))


MLP_FF_CHUNK = 1024


def _hyena_out_mlp(x2, x0, u, yc, mod_l, mod_index, seq_len, skip, w_bf, b, g, w1_bf, w2_bf):
    t_rows, d = x2.shape
    ff = w1_bf.shape[1]
    tm = min(512, seq_len)
    tps = seq_len // tm
    row = pl.BlockSpec((tm, d), lambda t: (t, 0))
    return pl.pallas_call(
        functools.partial(_hy_out_mlp_kernel, ff_chunk=MLP_FF_CHUNK),
        grid=(t_rows // tm,),
        in_specs=[row, row, row, row,
                  pl.BlockSpec((None, MOD_CHUNKS, d), lambda t: (mod_index(t, tps), 0, 0)),
                  _full((1, d)), _resident((d, d)), _full((1, d)),
                  _full((1, d)), _resident((d, ff)), _resident((ff, d))],
        out_specs=row,
        out_shape=jax.ShapeDtypeStruct((t_rows, d), F32),
        compiler_params=_params(1),
        name="hyena_out_mlp",
    )(x2, x0, u, yc, mod_l, skip.reshape(1, d), w_bf, b.reshape(1, d), g.reshape(1, d), w1_bf, w2_bf)


def _mlp_kernel(x_ref, mod_ref, g_ref, w1_ref, w2_ref, o_ref, *, ff_chunk):
    o_ref[...] = _mlp_body(x_ref[...], mod_ref, g_ref, w1_ref, w2_ref, ff_chunk)


def _mlp(x2, mod_l, mod_index, seq_len, g, w1_bf, w2_bf):
    t_rows, d = x2.shape
    ff = w1_bf.shape[1]
    tm = min(1024, seq_len)
    tps = seq_len // tm
    row = pl.BlockSpec((tm, d), lambda t: (t, 0))
    return pl.pallas_call(
        functools.partial(_mlp_kernel, ff_chunk=MLP_FF_CHUNK),
        grid=(t_rows // tm,),
        in_specs=[row,
                  pl.BlockSpec((None, MOD_CHUNKS, d), lambda t: (mod_index(t, tps), 0, 0)),
                  _full((1, d)), _resident((d, ff)), _resident((ff, d))],
        out_specs=row,
        out_shape=jax.ShapeDtypeStruct((t_rows, d), F32),
        compiler_params=_params(1),
        name="mlp",
    )(x2, mod_l, g.reshape(1, d), w1_bf, w2_bf)


def _rope(v, cos, sin_signed, hi_half):
    partner = jnp.where(hi_half, pltpu.roll(v, 16, axis=1), pltpu.roll(v, LANES - 16, axis=1))
    return v * cos + partner * sin_signed


def _qkv_kernel(*refs, with_q, with_rope, q_width, kv_width):
    x_ref, mod_ref, g_ref, w_ref, b_ref, bd_ref, qn_ref, kn_ref = refs[:8]
    idx = 8
    if with_rope:
        cos_ref, sin_ref = refs[idx:idx + 2]
        idx += 2
    outs = refs[idx:]
    if with_q:
        q_ref, kt_ref, v_ref = outs
    else:
        kt_ref, v_ref = outs
    h = _norm_mod(x_ref[...], g_ref[...], mod_ref[0:1, :], mod_ref[1:2, :]).astype(BF16)
    tm = h.shape[0]
    if with_rope:
        cos = cos_ref[...]
        sin = sin_ref[...]
        hi_half = (lax.broadcasted_iota(jnp.int32, (tm, LANES), 1) & 16) != 0

    def head_norm(p, w, bd):
        sq = (p * p).astype(BF16)
        sw = bd.shape[0]
        ms = jnp.concatenate([_dot(sq[:, c0:c0 + sw], bd) for c0 in range(0, p.shape[1], sw)], axis=1)
        return p * lax.rsqrt(ms + EPS) * w

    def rotate(p):
        if not with_rope:
            return p
        return jnp.concatenate([_rope(p[:, c0:c0 + LANES], cos, sin, hi_half)
                                for c0 in range(0, p.shape[1], LANES)], axis=1)

    off = 0
    if with_q:
        q = _dot(h, w_ref[:, 0:q_width]) + b_ref[:, 0:q_width]
        q_ref[...] = rotate(head_norm(q, qn_ref[...], bd_ref[...]) * Q_SCALE_LOG2).astype(q_ref.dtype)
        off = q_width
    k = _dot(h, w_ref[:, off:off + kv_width]) + b_ref[:, off:off + kv_width]
    k = rotate(head_norm(k, kn_ref[...], bd_ref[...]))
    kt_ref[...] = k.T.astype(kt_ref.dtype)
    v = _dot(h, w_ref[:, off + kv_width:off + 2 * kv_width]) + b_ref[:, off + kv_width:off + 2 * kv_width]
    v_ref[...] = v.astype(v_ref.dtype)


def _rope_tables(seq_len):
    rows = seq_len // GRID_W
    row = jnp.repeat(jnp.arange(rows, dtype=F32), GRID_W)
    col = jnp.tile(jnp.arange(GRID_W, dtype=F32), rows)
    inv = ROPE_BASE ** (-jnp.arange(ROPE_NFREQ, dtype=F32) / ROPE_NFREQ)
    ar = row[:, None] * inv[None, :]
    ac = col[:, None] * inv[None, :]
    cos = jnp.concatenate([jnp.cos(ar), jnp.cos(ar), jnp.cos(ac), jnp.cos(ac)], axis=-1)
    sin = jnp.concatenate([-jnp.sin(ar), jnp.sin(ar), -jnp.sin(ac), jnp.sin(ac)], axis=-1)
    return jnp.tile(cos, (1, LANES // HEAD_DIM)), jnp.tile(sin, (1, LANES // HEAD_DIM))


def _qkv_proj(x2, mod_l, mod_index, seq_len, g, w_bf, b, qn, kn, with_q, rope):
    t_rows, d = x2.shape
    q_width = N_HEADS * HEAD_DIM
    kv_width = N_KV_HEADS * HEAD_DIM
    width = w_bf.shape[1]
    tm = min(1024, seq_len)
    tps = seq_len // tm
    slab_heads = V7X_MXU_DIM // HEAD_DIM
    bd = jnp.asarray(np.kron(np.eye(slab_heads), np.full((HEAD_DIM, HEAD_DIM), 1.0 / HEAD_DIM)).astype(np.float32))
    row = lambda w: pl.BlockSpec((tm, w), lambda t: (t, 0))
    in_specs = [row(d),
                pl.BlockSpec((None, MOD_CHUNKS, d), lambda t: (mod_index(t, tps), 0, 0)),
                _full((1, d)), _full((d, width)), _full((1, width)), _full((V7X_MXU_DIM, V7X_MXU_DIM)),
                _full((1, q_width)), _full((1, kv_width))]
    args = [x2, mod_l, g.reshape(1, d), w_bf, b.reshape(1, width), bd.astype(BF16),
            jnp.tile(qn, N_HEADS).reshape(1, q_width), jnp.tile(kn, N_KV_HEADS).reshape(1, kv_width)]
    if rope is not None:
        in_specs += [pl.BlockSpec((tm, LANES), lambda t: (t % tps, 0))] * 2
        args += list(rope)
    out_specs = [pl.BlockSpec((kv_width, tm), lambda t: (0, t)), row(kv_width)]
    out_shape = [jax.ShapeDtypeStruct((kv_width, t_rows), BF16), jax.ShapeDtypeStruct((t_rows, kv_width), BF16)]
    if with_q:
        out_specs = [row(q_width)] + out_specs
        out_shape = [jax.ShapeDtypeStruct((t_rows, q_width), BF16)] + out_shape
    return pl.pallas_call(
        functools.partial(_qkv_kernel, with_q=with_q, with_rope=rope is not None,
                          q_width=q_width, kv_width=kv_width),
        grid=(t_rows // tm,),
        in_specs=in_specs, out_specs=out_specs, out_shape=out_shape,
        compiler_params=_params(1),
        name="qkv_proj",
    )(*args)


def _attn_kernel(sink_ref, x_ref, q_ref, ktp_ref, ktm_ref, ktn_ref, ktc_ref, vp_ref, vm_ref, vn_ref, vc_ref,
                 mod_ref, w_ref, b_ref, o_ref, att_ref, *, tq, n_qt):
    i = pl.program_id(1)
    lc = vc_ref.shape[0]
    qb = min(ATT_QB, tq)
    n_sub = qb // WINDOW
    nw = qb + 2 * WINDOW
    nk = nw + lc
    sw = 3 * WINDOW
    sk = sw + lc
    n_rb = tq // qb
    r = lax.broadcasted_iota(jnp.int32, (WINDOW, WINDOW), 0)
    c = lax.broadcasted_iota(jnp.int32, (WINDOW, WINDOW), 1)
    before_bias = jnp.where(c >= r, 0.0, NEG_BIG)
    after_bias = jnp.where(c <= r, 0.0, NEG_BIG)
    first_before = jnp.where(jnp.logical_and(c >= r, i > 0), 0.0, NEG_BIG)
    last_after = jnp.where(jnp.logical_and(c <= r, i < n_qt - 1), 0.0, NEG_BIG)

    def masked_scores(raw, rb, j):
        rows = slice(j * WINDOW, (j + 1) * WINDOW)
        lo = j * WINDOW
        bb = first_before if (rb == 0 and j == 0) else before_bias
        ba = last_after if (rb == n_rb - 1 and j == n_sub - 1) else after_bias
        return jnp.concatenate([raw[rows, lo:lo + WINDOW] + bb,
                                raw[rows, lo + WINDOW:lo + 2 * WINDOW],
                                raw[rows, lo + 2 * WINDOW:lo + sw] + ba,
                                raw[rows, nw:nk]], axis=1)

    zeros = jnp.zeros((HEAD_DIM, nk), BF16)
    for hk in range(N_KV_HEADS):
        hs = slice(hk * HEAD_DIM, (hk + 1) * HEAD_DIM)
        ktw = jnp.concatenate([ktp_ref[hs, :], ktm_ref[hs, :], ktn_ref[hs, :]], axis=1)
        vw = jnp.concatenate([vp_ref[:, hs], vm_ref[:, hs], vn_ref[:, hs]], axis=0)
        ktc = ktc_ref[hs, :]
        vcx = vc_ref[:, hs]
        heads = [hk * GROUP + gq for gq in range(GROUP)]
        for rb in range(n_rb):
            rows = slice(rb * qb, (rb + 1) * qb)
            kk = jnp.concatenate([ktw[:, rb * qb:rb * qb + nw], ktc], axis=1)
            vv = jnp.concatenate([vw[rb * qb:rb * qb + nw], vcx], axis=0)
            kk_pair = (jnp.concatenate([kk, zeros], axis=0), jnp.concatenate([zeros, kk], axis=0))
            raw = [_dot(q_ref[rows, (hd // 2) * LANES:(hd // 2 + 1) * LANES], kk_pair[hd % 2]) for hd in heads]
            chains = [(gq, j) for gq in range(GROUP) for j in range(n_sub)]
            ss = [masked_scores(raw[gq], rb, j) for gq, j in chains]
            ms = [jnp.maximum(jnp.max(s, axis=-1, keepdims=True), sink_ref[heads[gq]])
                  for s, (gq, j) in zip(ss, chains)]
            ps = [jnp.exp2(s - m) for s, m in zip(ss, ms)]
            dens = [jnp.sum(p, axis=-1, keepdims=True) + jnp.exp2(sink_ref[heads[gq]] - m)
                    for p, m, (gq, j) in zip(ps, ms, chains)]
            for gq, hd in enumerate(heads):
                full = []
                for j in range(n_sub):
                    p = ps[gq * n_sub + j].astype(BF16)
                    parts = [p[:, 0:sw], p[:, sw:sk]]
                    if j > 0:
                        parts.insert(0, jnp.zeros((WINDOW, j * WINDOW), BF16))
                    if nw - sw - j * WINDOW > 0:
                        parts.insert(-1, jnp.zeros((WINDOW, nw - sw - j * WINDOW), BF16))
                    full.append(jnp.concatenate(parts, axis=1))
                den = jnp.concatenate(dens[gq * n_sub:(gq + 1) * n_sub], axis=0)
                o = _dot(jnp.concatenate(full, axis=0), vv) / den
                att_ref[rows, hd * HEAD_DIM:(hd + 1) * HEAD_DIM] = o.astype(BF16)
    dx = _dot(att_ref[...], w_ref[...]) + b_ref[...]
    o_ref[...] = x_ref[...] + mod_ref[2:3, :] * dx


def _window_attention(x2, q, kt, v, ktc, vc, sink, mod_l, batch, seq_len, w_bf, b):
    t_rows, d = x2.shape
    kvw = v.shape[1]
    lc = vc.shape[0] // batch
    tq = min(512, seq_len)
    n_qt = seq_len // tq
    wpt = tq // WINDOW
    nwb = t_rows // WINDOW
    tile = lambda bi, i: bi * n_qt + i
    prev_blk = lambda bi, i: jnp.maximum(tile(bi, i) * wpt - 1, 0)
    next_blk = lambda bi, i: jnp.minimum((tile(bi, i) + 1) * wpt, nwb - 1)
    qrow = lambda w: pl.BlockSpec((tq, w), lambda bi, i: (tile(bi, i), 0))
    return pl.pallas_call(
        functools.partial(_attn_kernel, tq=tq, n_qt=n_qt),
        grid=(batch, n_qt),
        in_specs=[pl.BlockSpec(memory_space=pltpu.SMEM),
                  qrow(d), qrow(q.shape[1]),
                  pl.BlockSpec((kvw, WINDOW), lambda bi, i: (0, prev_blk(bi, i))),
                  pl.BlockSpec((kvw, tq), lambda bi, i: (0, tile(bi, i))),
                  pl.BlockSpec((kvw, WINDOW), lambda bi, i: (0, next_blk(bi, i))),
                  pl.BlockSpec((kvw, lc), lambda bi, i: (0, bi)),
                  pl.BlockSpec((WINDOW, kvw), lambda bi, i: (prev_blk(bi, i), 0)),
                  qrow(kvw),
                  pl.BlockSpec((WINDOW, kvw), lambda bi, i: (next_blk(bi, i), 0)),
                  pl.BlockSpec((lc, kvw), lambda bi, i: (bi, 0)),
                  pl.BlockSpec((None, MOD_CHUNKS, d), lambda bi, i: (bi, 0, 0)),
                  _resident((d, d)), pl.BlockSpec((1, d), lambda bi, i: (0, 0))],
        out_specs=qrow(d),
        out_shape=jax.ShapeDtypeStruct((t_rows, d), F32),
        scratch_shapes=[pltpu.VMEM((tq, d), BF16)],
        compiler_params=_params(2),
        name="window_attention",
    )(sink, x2, q, kt, kt, kt, ktc, v, v, v, vc, mod_l, w_bf, b.reshape(1, d))


def _latent_mod_index(t, tiles_per_seq):
    return t // tiles_per_seq


def kernel(x, c, ctx, c_ctx, mod_w, mod_b, norm1_w, norm2_w, mlp_w1, mlp_w2, hy_w_in, hy_b_in, hy_conv_w,
           hy_conv_b, hy_f_w1, hy_f_b1, hy_f_freq1, hy_f_w2, hy_f_b2, hy_f_freq2, hy_f_w3, hy_skip, hy_w_out,
           hy_b_out, at_w_qkv, at_b_qkv, at_q_norm, at_k_norm, at_sink, at_w_out, at_b_out):
    batch, seq_len, d = x.shape
    lc = ctx.shape[1]
    ctx_row = batch

    def ctx_mod_index(t, tiles_per_seq):
        return ctx_row

    cond_rows = -(-(batch + 1) // 8) * 8
    cc = jnp.zeros((cond_rows, d), F32).at[:batch].set(c).at[batch].set(c_ctx)
    mod = _modulation(cc, mod_w, mod_b)

    x2 = x.reshape(batch * seq_len, d)
    c2 = ctx.reshape(batch * lc, d)

    w_in = hy_w_in[0].astype(BF16)
    w_out = hy_w_out[0].astype(BF16)
    w1 = mlp_w1[0].astype(BF16)
    w2 = mlp_w2[0].astype(BF16)
    fpar = (hy_f_w1[0], hy_f_b1[0], hy_f_freq1[0], hy_f_w2[0], hy_f_b2[0], hy_f_freq2[0], hy_f_w3[0])
    def hyena_layer(tok, n_seq, idx_fn, conv):
        x0, u = _hyena_inproj(tok, mod[0], idx_fn, n_seq, norm1_w[0], w_in, hy_b_in[0], hy_conv_w[0],
                              hy_conv_b[0])
        kc, nrm = _hyena_filter(n_seq, d, *fpar)
        yc = conv(u.reshape(batch, n_seq, d), kc, nrm).reshape(batch * n_seq, d)
        return _hyena_out_mlp(tok, x0, u, yc, mod[0], idx_fn, n_seq, hy_skip[0], w_out, hy_b_out[0],
                              norm2_w[0], w1, w2)

    x2 = hyena_layer(x2, seq_len, _latent_mod_index, _long_conv)
    c2 = hyena_layer(c2, lc, ctx_mod_index, _short_seq_conv)

    w_qkv = at_w_qkv[0].astype(BF16)
    q_width = N_HEADS * HEAD_DIM
    rope = _rope_tables(seq_len)
    q, kt, v = _qkv_proj(x2, mod[1], _latent_mod_index, seq_len, norm1_w[1], w_qkv, at_b_qkv[0],
                         at_q_norm[0], at_k_norm[0], True, rope)
    kt_ctx, v_ctx = _qkv_proj(c2, mod[1], ctx_mod_index, lc, norm1_w[1], w_qkv[:, q_width:],
                              at_b_qkv[0][q_width:], at_q_norm[0], at_k_norm[0], False, None)
    x2 = _window_attention(x2, q, kt, v, kt_ctx, v_ctx, at_sink[0].astype(F32) * LOG2_E, mod[1], batch,
                           seq_len, at_w_out[0].astype(BF16), at_b_out[0])
    x2 = _mlp(x2, mod[1], _latent_mod_index, seq_len, norm2_w[1], mlp_w1[1].astype(BF16),
              mlp_w2[1].astype(BF16))
    return x2.reshape(batch, seq_len, d)
```
